```python
import math
import jax, jax.numpy as jnp
from jax import lax
import numpy as np

D_MODEL = 4096
BATCH = 2
SEQ = 8192
DEPTH = 2

HEAD_DIM = 128
H_A = (3 * D_MODEL) // (8 * HEAD_DIM)
H_B = D_MODEL // (4 * HEAD_DIM)
H_C = (3 * D_MODEL) // (8 * HEAD_DIM)
D_A = H_A * HEAD_DIM
D_B = H_B * HEAD_DIM
D_C = H_C * HEAD_DIM
D_MIX = D_A + D_B + D_C
SPLIT_SIZES = (3 * D_A, D_A, H_A, H_A, 2 * D_B, 3 * D_C, H_C)
D_IN = sum(SPLIT_SIZES)
CONV_K = 4
DN_CHUNK = 64
SGU_CHUNK = 128
FOX_BLOCK = 128
D_FF = 2 * D_MODEL
N_MOD = 9
ALPHA = (2.0 * DEPTH) ** 0.25
BETA_INIT = (8.0 * DEPTH) ** -0.25
LN_EPS = 1e-5
RMS_EPS = 1e-6

kernel_name = "hybrid_deltanet_gmlp_fox_macaron_deepnorm"


def layer_norm(x, g, b):
    xf = x.astype(jnp.float32)
    mu = jnp.mean(xf, -1, keepdims=True)
    var = jnp.mean(jnp.square(xf - mu), -1, keepdims=True)
    return ((xf - mu) * lax.rsqrt(var + LN_EPS) * g.astype(jnp.float32) + b.astype(jnp.float32)).astype(x.dtype)


def rms_norm(x, g):
    xf = x.astype(jnp.float32)
    return xf * lax.rsqrt(jnp.mean(jnp.square(xf), -1, keepdims=True) + RMS_EPS) * g.astype(jnp.float32)


def l2_normalize(x):
    return x * lax.rsqrt(jnp.sum(jnp.square(x), -1, keepdims=True) + RMS_EPS)


def modulate(x, shift, scale):
    return x * (1.0 + scale[:, None, :]) + shift[:, None, :]


def swiglu_ffn(h, w_in, w_out):
    gate, up = jnp.split(h @ w_in, 2, axis=-1)
    return (jax.nn.silu(gate) * up) @ w_out


def causal_depthwise_conv(x, w):
    return lax.conv_general_dilated(
        x, w[:, None, :].astype(x.dtype), window_strides=(1,), padding=((CONV_K - 1, 0),),
        dimension_numbers=("NWC", "WIO", "NWC"), feature_group_count=x.shape[-1])


def gated_delta_rule(q, k, v, g, beta):
    bsz, seq, nh, dh = q.shape
    n = seq // DN_CHUNK
    C = DN_CHUNK

    def chunks(t):
        return jnp.moveaxis(t.reshape(bsz, n, C, nh, *t.shape[3:]), 3, 1)

    q, k, v, g, beta = chunks(q) * dh ** -0.5, chunks(k), chunks(v), chunks(g), chunks(beta)
    g = jnp.cumsum(g, axis=-1)
    tri = jnp.tril(jnp.ones((C, C), bool))
    strict = jnp.tril(jnp.ones((C, C), bool), -1)
    diff = g[..., :, None] - g[..., None, :]
    decay = jnp.exp(jnp.where(tri, diff, -jnp.inf))
    k_beta = k * beta[..., None]
    v_beta = v * beta[..., None]
    kk = jnp.einsum("bhncd,bhnsd->bhncs", k_beta, k) * decay
    lower = jnp.where(strict, kk, 0.0) + jnp.eye(C, dtype=jnp.float32)
    rhs = jnp.concatenate([v_beta, k_beta * jnp.exp(g)[..., None]], axis=-1)
    sol = lax.linalg.triangular_solve(lower, rhs, left_side=True, lower=True)
    u, w = sol[..., :dh], sol[..., dh:]
    qk = jnp.einsum("bhncd,bhnsd->bhncs", q, k) * decay

    def step(state, inp):
        q_i, k_i, u_i, w_i, qk_i, g_i = inp
        v_new = u_i - jnp.einsum("bhcd,bhde->bhce", w_i, state)
        o = (jnp.einsum("bhcd,bhde->bhce", q_i * jnp.exp(g_i)[..., None], state)
             + jnp.einsum("bhcs,bhse->bhce", qk_i, v_new))
        g_last = g_i[..., -1]
        state = (state * jnp.exp(g_last)[..., None, None]
                 + jnp.einsum("bhcd,bhce->bhde", k_i * jnp.exp(g_last[..., None] - g_i)[..., None], v_new))
        return state, o

    xs = tuple(jnp.moveaxis(t, 2, 0) for t in (q, k, u, w, qk, g))
    s0 = jnp.zeros((bsz, nh, dh, dh), jnp.float32)
    _, o = lax.scan(step, s0, xs)
    return jnp.transpose(o, (1, 0, 3, 2, 4)).reshape(bsz, seq, nh, dh)


def spatial_gating(uv, ln_g, ln_b, w_s, b_s):
    u, v = jnp.split(uv, 2, axis=-1)
    v = layer_norm(v, ln_g, ln_b)
    bsz, seq, _ = v.shape
    n = seq // SGU_CHUNK
    vc = v.reshape(bsz, n, SGU_CHUNK, H_B, HEAD_DIM)
    w_causal = jnp.where(jnp.tril(jnp.ones((SGU_CHUNK, SGU_CHUNK), bool)), w_s, 0.0).astype(v.dtype)
    mixed = jnp.einsum("gts,bnsgc->bntgc", w_causal, vc) + b_s.T[None, None, :, :, None].astype(v.dtype)
    return u * mixed.reshape(bsz, seq, D_B)


def forgetting_attention(q, k, v, log_f):
    seq = q.shape[1]
    dh = q.shape[-1]
    cum = jnp.cumsum(log_f, axis=1)
    cum_h = jnp.transpose(cum, (0, 2, 1))
    outs = []
    for i in range(seq // FOX_BLOCK):
        lo, hi = i * FOX_BLOCK, (i + 1) * FOX_BLOCK
        logits = jnp.einsum("bthd,bshd->bhts", q[:, lo:hi], k[:, :hi],
                            preferred_element_type=jnp.float32) * dh ** -0.5
        logits = logits + cum_h[:, :, lo:hi, None] - cum_h[:, :, None, :hi]
        mask = jnp.arange(hi)[None, :] <= (lo + jnp.arange(FOX_BLOCK))[:, None]
        p = jax.nn.softmax(jnp.where(mask, logits, -jnp.inf), axis=-1)
        outs.append(jnp.einsum("bhts,bshd->bthd", p.astype(v.dtype), v[:, :hi]))
    return jnp.concatenate(outs, axis=1)


def hybrid_mixer(h, w_in, conv_w, a_log, dt_bias, norm_a, sgu_ln_g, sgu_ln_b, w_s, b_s, b_f, norm_c, w_o):
    bsz, seq, _ = h.shape
    points = [int(p) for p in np.cumsum(SPLIT_SIZES)[:-1]]
    qkv_a, z_a, beta_a, dec_a, uv_b, qkv_c, f_c = jnp.split(h @ w_in, points, axis=-1)
    qkv_a = jax.nn.silu(causal_depthwise_conv(qkv_a, conv_w))
    q_a, k_a, v_a = [t.reshape(bsz, seq, H_A, HEAD_DIM).astype(jnp.float32) for t in jnp.split(qkv_a, 3, axis=-1)]
    beta = jax.nn.sigmoid(beta_a.astype(jnp.float32))
    g = -jnp.exp(a_log.astype(jnp.float32)) * jax.nn.softplus(dec_a.astype(jnp.float32) + dt_bias.astype(jnp.float32))
    o_a = gated_delta_rule(l2_normalize(q_a), l2_normalize(k_a), v_a, g, beta)
    o_a = rms_norm(o_a, norm_a) * jax.nn.silu(z_a.reshape(bsz, seq, H_A, HEAD_DIM).astype(jnp.float32))
    o_a = o_a.reshape(bsz, seq, D_A).astype(h.dtype)
    o_b = spatial_gating(jax.nn.gelu(uv_b, approximate=False), sgu_ln_g, sgu_ln_b, w_s, b_s)
    q_c, k_c, v_c = [t.reshape(bsz, seq, H_C, HEAD_DIM) for t in jnp.split(qkv_c, 3, axis=-1)]
    log_f = jax.nn.log_sigmoid(f_c.astype(jnp.float32) + b_f.astype(jnp.float32))
    o_c = rms_norm(forgetting_attention(q_c, k_c, v_c, log_f), norm_c).reshape(bsz, seq, D_C).astype(h.dtype)
    return jnp.concatenate([o_a, o_b, o_c], axis=-1) @ w_o


def setup_inputs(seed: int = 0) -> dict:
    key = jax.random.key(seed)
    ks = jax.random.split(key, 20)
    f32 = jnp.float32
    nrm = lambda k, shape, s: jax.random.normal(k, shape, f32) * s
    dt = jnp.exp(jax.random.uniform(ks[10], (DEPTH, H_A), f32, math.log(1e-3), math.log(1e-1)))
    return {
        "x": nrm(ks[0], (BATCH, SEQ, D_MODEL), 1.0),
        "c": nrm(ks[1], (BATCH, D_MODEL), 1.0),
        "w_ada": nrm(ks[2], (DEPTH, D_MODEL, N_MOD * D_MODEL), 0.5 * D_MODEL ** -0.5),
        "b_ada": nrm(ks[3], (DEPTH, N_MOD * D_MODEL), 0.02),
        "ln_g": 1.0 + nrm(ks[4], (DEPTH, 3, D_MODEL), 0.02),
        "ln_b": nrm(ks[5], (DEPTH, 3, D_MODEL), 0.02),
        "w_ffn_in": nrm(ks[6], (DEPTH, 2, D_MODEL, 2 * D_FF), D_MODEL ** -0.5),
        "w_ffn_out": nrm(ks[7], (DEPTH, 2, D_FF, D_MODEL), BETA_INIT * D_FF ** -0.5),
        "w_in": nrm(ks[8], (DEPTH, D_MODEL, D_IN), D_MODEL ** -0.5),
        "conv_w": nrm(ks[9], (DEPTH, CONV_K, 3 * D_A), CONV_K ** -0.5),
        "a_log": jnp.log(jax.random.uniform(ks[11], (DEPTH, H_A), f32, 1.0, 16.0)),
        "dt_bias": dt + jnp.log(-jnp.expm1(-dt)),
        "norm_a": 1.0 + nrm(ks[12], (DEPTH, HEAD_DIM), 0.02),
        "sgu_ln_g": 1.0 + nrm(ks[13], (DEPTH, D_B), 0.02),
        "sgu_ln_b": nrm(ks[14], (DEPTH, D_B), 0.02),
        "w_s": nrm(ks[15], (DEPTH, H_B, SGU_CHUNK, SGU_CHUNK), 0.5 * SGU_CHUNK ** -0.5),
        "b_s": 1.0 + nrm(ks[16], (DEPTH, H_B, SGU_CHUNK), 0.1),
        "b_f": 4.0 + nrm(ks[17], (DEPTH, H_C), 0.5),
        "norm_c": 1.0 + nrm(ks[18], (DEPTH, HEAD_DIM), 0.02),
        "w_o": nrm(ks[19], (DEPTH, D_MIX, D_MODEL), BETA_INIT * D_MIX ** -0.5),
    }


def reference(x, c, w_ada, b_ada, ln_g, ln_b, w_ffn_in, w_ffn_out, w_in, conv_w, a_log, dt_bias,
              norm_a, sgu_ln_g, sgu_ln_b, w_s, b_s, b_f, norm_c, w_o):
    c_act = jax.nn.silu(c)
    for l in range(DEPTH):
        mod = c_act @ w_ada[l] + b_ada[l]
        sh1, sc1, ga1, sh2, sc2, ga2, sh3, sc3, ga3 = jnp.split(mod, N_MOD, axis=-1)
        y = swiglu_ffn(modulate(x, sh1, sc1), w_ffn_in[l, 0], w_ffn_out[l, 0])
        x = layer_norm(ALPHA * x + 0.5 * ga1[:, None, :] * y, ln_g[l, 0], ln_b[l, 0])
        y = hybrid_mixer(modulate(x, sh2, sc2), w_in[l], conv_w[l], a_log[l], dt_bias[l], norm_a[l],
                         sgu_ln_g[l], sgu_ln_b[l], w_s[l], b_s[l], b_f[l], norm_c[l], w_o[l])
        x = layer_norm(ALPHA * x + ga2[:, None, :] * y, ln_g[l, 1], ln_b[l, 1])
        y = swiglu_ffn(modulate(x, sh3, sc3), w_ffn_in[l, 1], w_ffn_out[l, 1])
        x = layer_norm(ALPHA * x + 0.5 * ga3[:, None, :] * y, ln_g[l, 2], ln_b[l, 2])
    return x
```

```python
import functools
import math

import jax
import jax.numpy as jnp
from jax import lax
from jax.experimental import pallas as pl
from jax.experimental.pallas import tpu as pltpu

F32 = jnp.float32
BF16 = jnp.bfloat16

HEAD_DIM = 128
DN_CHUNK = 64
LN_EPS = 1e-5
RMS_EPS = 1e-6
LANES = 128
SUBLANES = 8
VMEM_LIMIT = 56 * 1024 * 1024


def _params(sem, vmem=VMEM_LIMIT):
    return pltpu.CompilerParams(dimension_semantics=sem, vmem_limit_bytes=vmem)


def _tile(n, pref):
    t = min(n, pref)
    while n % t:
        t //= 2
    return t


def _silu(x):
    return x * jax.nn.sigmoid(x)


def _softplus(x):
    return jnp.maximum(x, 0.0) + jnp.log1p(jnp.exp(-jnp.abs(x)))


def _dot(a, b):
    return jnp.dot(a, b, preferred_element_type=F32)


def _dot_nt(a, b):
    return lax.dot_general(a, b, (((1,), (1,)), ((), ())), preferred_element_type=F32)


def _dot_tn(a, b):
    return lax.dot_general(a, b, (((0,), (0,)), ((), ())), preferred_element_type=F32)


def _ada_kernel(c_ref, w_ref, b_ref, o_ref):
    c = c_ref[...]
    o_ref[...] = _dot(_silu(c).astype(BF16), w_ref[...].astype(BF16)) + b_ref[...]


def _ada(c_pad, w_ada, b_ada):
    depth, d, n = w_ada.shape
    bp = c_pad.shape[0]
    tn = _tile(n, 512)
    return pl.pallas_call(
        _ada_kernel,
        grid=(depth, n // tn),
        in_specs=[
            pl.BlockSpec((bp, d), lambda l, j: (0, 0)),
            pl.BlockSpec((None, d, tn), lambda l, j: (l, 0, j)),
            pl.BlockSpec((None, 1, tn), lambda l, j: (l, 0, j)),
        ],
        out_specs=pl.BlockSpec((None, bp, tn), lambda l, j: (l, 0, j)),
        out_shape=jax.ShapeDtypeStruct((depth, bp, n), F32),
        compiler_params=_params(("parallel", "parallel")),
        name="ada_mod",
    )(c_pad, w_ada, b_ada.reshape(depth, 1, n))


def _modulate_kernel(x_ref, sh_ref, sc_ref, h_ref):
    h_ref[...] = (x_ref[...] * (1.0 + sc_ref[...]) + sh_ref[...]).astype(h_ref.dtype)


def _modulate(x2, sh, sc, seq):
    m, d = x2.shape
    tm = _tile(seq, 512)
    per = seq // tm
    vec = pl.BlockSpec((None, 1, d), lambda i: (i // per, 0, 0))
    return pl.pallas_call(
        _modulate_kernel,
        grid=(m // tm,),
        in_specs=[pl.BlockSpec((tm, d), lambda i: (i, 0)), vec, vec],
        out_specs=pl.BlockSpec((tm, d), lambda i: (i, 0)),
        out_shape=jax.ShapeDtypeStruct((m, d), BF16),
        compiler_params=_params(("parallel",)),
        name="modulate",
    )(x2, sh, sc)


def _mm_kernel(a_ref, w_ref, o_ref):
    o_ref[...] = _dot(a_ref[...], w_ref[...]).astype(o_ref.dtype)


def _matmul(a, w, out_dtype, tm=1024, tn=1024, name="matmul"):
    m, k = a.shape
    n = w.shape[1]
    tm, tn = _tile(m, tm), _tile(n, tn)
    return pl.pallas_call(
        _mm_kernel,
        grid=(m // tm, n // tn),
        in_specs=[pl.BlockSpec((tm, k), lambda i, j: (i, 0)),
                  pl.BlockSpec((k, tn), lambda i, j: (0, j))],
        out_specs=pl.BlockSpec((tm, tn), lambda i, j: (i, j)),
        out_shape=jax.ShapeDtypeStruct((m, n), out_dtype),
        compiler_params=_params(("parallel", "parallel")),
        name=name,
    )(a, w)


def _swiglu_kernel(a_ref, wg_ref, wu_ref, o_ref):
    a = a_ref[...]
    g = _dot(a, wg_ref[...])
    u = _dot(a, wu_ref[...])
    o_ref[...] = (_silu(g) * u).astype(o_ref.dtype)


def _swiglu_in(h, w_in, tm=1024, tf=512):
    m, k = h.shape
    f = w_in.shape[1] // 2
    tm, tf = _tile(m, tm), _tile(f, tf)
    nf = f // tf
    return pl.pallas_call(
        _swiglu_kernel,
        grid=(m // tm, nf),
        in_specs=[pl.BlockSpec((tm, k), lambda i, j: (i, 0)),
                  pl.BlockSpec((k, tf), lambda i, j: (0, j)),
                  pl.BlockSpec((k, tf), lambda i, j: (0, j + nf))],
        out_specs=pl.BlockSpec((tm, tf), lambda i, j: (i, j)),
        out_shape=jax.ShapeDtypeStruct((m, f), BF16),
        compiler_params=_params(("parallel", "parallel")),
        name="ffn_in_swiglu",
    )(h, w_in, w_in)


def _mm_acc_kernel(a_ref, w_ref, o_ref, acc_ref):
    kk = pl.program_id(2)

    @pl.when(kk == 0)
    def _():
        acc_ref[...] = jnp.zeros_like(acc_ref)

    acc_ref[...] += _dot(a_ref[...], w_ref[...])

    @pl.when(kk == pl.num_programs(2) - 1)
    def _():
        o_ref[...] = acc_ref[...].astype(o_ref.dtype)


def _matmul_ktiled(a, w, out_dtype, tm=1024, tn=1024, tk=2048, name="matmul_k"):
    m, k = a.shape
    n = w.shape[1]
    tm, tn, tk = _tile(m, tm), _tile(n, tn), _tile(k, tk)
    return pl.pallas_call(
        _mm_acc_kernel,
        grid=(m // tm, n // tn, k // tk),
        in_specs=[pl.BlockSpec((tm, tk), lambda i, j, q: (i, q)),
                  pl.BlockSpec((tk, tn), lambda i, j, q: (q, j))],
        out_specs=pl.BlockSpec((tm, tn), lambda i, j, q: (i, j)),
        out_shape=jax.ShapeDtypeStruct((m, n), out_dtype),
        scratch_shapes=[pltpu.VMEM((tm, tn), F32)],
        compiler_params=_params(("parallel", "parallel", "arbitrary")),
        name=name,
    )(a, w)


def _resid_ln_kernel(x_ref, y_ref, ga_ref, g_ref, b_ref, *rest, alpha, coef, with_next):
    if with_next:
        sh_ref, sc_ref, xo_ref, h_ref = rest
    else:
        (xo_ref,) = rest
    r = alpha * x_ref[...] + (coef * ga_ref[...]) * y_ref[...].astype(F32)
    mu = jnp.mean(r, axis=-1, keepdims=True)
    rc = r - mu
    var = jnp.mean(rc * rc, axis=-1, keepdims=True)
    xn = rc * lax.rsqrt(var + LN_EPS) * g_ref[...] + b_ref[...]
    xo_ref[...] = xn
    if with_next:
        h_ref[...] = (xn * (1.0 + sc_ref[...]) + sh_ref[...]).astype(h_ref.dtype)


def _resid_ln(x2, y, ga, ln_g, ln_b, nxt, seq, alpha, coef):
    m, d = x2.shape
    tm = _tile(seq, 256)
    per = seq // tm
    row = pl.BlockSpec((tm, d), lambda i: (i, 0))
    vec = pl.BlockSpec((None, 1, d), lambda i: (i // per, 0, 0))
    par = pl.BlockSpec((1, d), lambda i: (0, 0))
    with_next = nxt is not None
    in_specs = [row, row, vec, par, par]
    args = [x2, y, ga, ln_g.reshape(1, d), ln_b.reshape(1, d)]
    out_specs = [row]
    out_shape = [jax.ShapeDtypeStruct((m, d), F32)]
    if with_next:
        in_specs += [vec, vec]
        args += list(nxt)
        out_specs.append(row)
        out_shape.append(jax.ShapeDtypeStruct((m, d), BF16))
    res = pl.pallas_call(
        functools.partial(_resid_ln_kernel, alpha=alpha, coef=coef, with_next=with_next),
        grid=(m // tm,),
        in_specs=in_specs,
        out_specs=out_specs,
        out_shape=out_shape,
        compiler_params=_params(("parallel",)),
        name="resid_ln",
    )(*args)
    return (res[0], res[1]) if with_next else (res[0], None)


def _gates_kernel(s_ref, p_ref, g_ref, gt_ref, carry_ref, *, h_a, h_c, tile):
    t = pl.program_id(1)

    @pl.when(t == 0)
    def _():
        carry_ref[...] = jnp.zeros_like(carry_ref)

    x = s_ref[...]
    a_log = p_ref[0:1, :]
    dt_b = p_ref[1:2, :]
    b_f = p_ref[2:3, :]
    lane = lax.broadcasted_iota(jnp.int32, x.shape, 1)
    beta = jax.nn.sigmoid(x)
    g = -jnp.exp(a_log) * _softplus(x + dt_b)
    lf = -_softplus(-(x + b_f))
    is_g = (lane >= h_a) & (lane < 2 * h_a)
    is_f = (lane >= 2 * h_a) & (lane < 2 * h_a + h_c)
    g = jnp.where(is_g, g, 0.0)
    lf = jnp.where(is_f, lf, 0.0)
    r = lax.broadcasted_iota(jnp.int32, (tile, tile), 0)
    c = lax.broadcasted_iota(jnp.int32, (tile, tile), 1)
    tri = r >= c
    tril = jnp.where(tri, 1.0, 0.0).astype(F32)
    btril = jnp.where(tri & ((r // DN_CHUNK) == (c // DN_CHUNK)), 1.0, 0.0).astype(F32)
    hi = lax.Precision.HIGHEST
    gcum = jnp.dot(btril, g, precision=hi, preferred_element_type=F32)
    fcum = jnp.dot(tril, lf, precision=hi, preferred_element_type=F32) + carry_ref[0:1, :]
    carry_ref[0:1, :] = fcum[tile - 1:tile, :]
    out = jnp.where(lane < h_a, beta, jnp.where(is_g, gcum, jnp.where(is_f, fcum, 0.0)))
    g_ref[...] = out
    gt_ref[...] = out.T


def _gates(small, a_log, dt_bias, b_f, bsz, seq):
    m = small.shape[0]
    h_a, h_c = a_log.shape[0], b_f.shape[0]
    tile = _tile(seq, 512)
    nt = seq // tile
    p = jnp.zeros((SUBLANES, LANES), F32)
    p = p.at[0, h_a:2 * h_a].set(a_log).at[1, h_a:2 * h_a].set(dt_bias).at[2, 2 * h_a:2 * h_a + h_c].set(b_f)
    return pl.pallas_call(
        functools.partial(_gates_kernel, h_a=h_a, h_c=h_c, tile=tile),
        grid=(bsz, nt),
        in_specs=[pl.BlockSpec((tile, LANES), lambda b, t: (b * nt + t, 0)),
                  pl.BlockSpec((SUBLANES, LANES), lambda b, t: (0, 0))],
        out_specs=[pl.BlockSpec((tile, LANES), lambda b, t: (b * nt + t, 0)),
                   pl.BlockSpec((None, LANES, tile), lambda b, t: (b, 0, t))],
        out_shape=[jax.ShapeDtypeStruct((m, LANES), F32),
                   jax.ShapeDtypeStruct((bsz, LANES, seq), F32)],
        scratch_shapes=[pltpu.VMEM((SUBLANES, LANES), F32)],
        compiler_params=_params(("arbitrary", "arbitrary")),
        name="gates",
    )(small, p)


def _pick_lane(x, idx):
    lane = lax.broadcasted_iota(jnp.int32, x.shape, 1)
    return jnp.sum(jnp.where(lane == idx, x, 0.0), axis=-1, keepdims=True)


def _deltanet_kernel(q_ref, k_ref, v_ref, z_ref, wq_ref, wk_ref, wv_ref, g_ref, gt_ref, na_ref,
                     o_ref, state_ref, tail_ref, *, h_a, tile, conv_k):
    h = pl.program_id(1)
    t = pl.program_id(2)
    n_chunk = tile // DN_CHUNK

    @pl.when(t == 0)
    def _():
        state_ref[...] = jnp.zeros_like(state_ref)
        tail_ref[...] = jnp.zeros_like(tail_ref)

    row8 = lax.broadcasted_iota(jnp.int32, (SUBLANES, HEAD_DIM), 0)

    def conv_silu(x_ref, w_ref, slot):
        x = x_ref[...].astype(F32)
        w = w_ref[...]
        prev = tail_ref[slot]
        acc = x * w[conv_k - 1:conv_k, :]
        for sh in range(1, conv_k):
            xs = pltpu.roll(x, sh, 0)
            head = jnp.where(row8 < sh, pltpu.roll(prev, sh, 0), xs[:SUBLANES])
            xs = jnp.concatenate([head, xs[SUBLANES:]], axis=0)
            acc = acc + xs * w[conv_k - 1 - sh:conv_k - sh, :]
        tail_ref[slot] = x[tile - SUBLANES:, :]
        return _silu(acc)

    def l2n(x):
        return x * lax.rsqrt(jnp.sum(x * x, axis=-1, keepdims=True) + RMS_EPS)

    q = l2n(conv_silu(q_ref, wq_ref, 0)) * (HEAD_DIM ** -0.5)
    k = l2n(conv_silu(k_ref, wk_ref, 1))
    v = conv_silu(v_ref, wv_ref, 2)

    gates = g_ref[...]
    beta = _pick_lane(gates, h)
    gc = _pick_lane(gates, h_a + h)
    gr = gt_ref[pl.ds(h_a + h, 1), :]

    r = lax.broadcasted_iota(jnp.int32, (tile, tile), 0)
    c = lax.broadcasted_iota(jnp.int32, (tile, tile), 1)
    same = (r // DN_CHUNK) == (c // DN_CHUNK)
    tri = same & (r >= c)
    strict = same & (r > c)
    decay = jnp.where(tri, jnp.exp(jnp.where(tri, gc - gr, 0.0)), 0.0)

    kb = k * beta
    vb = v * beta
    k16 = k.astype(BF16)
    kk = _dot_nt(kb.astype(BF16), k16) * decay
    x = jnp.where(strict, -kk, 0.0)
    eye = jnp.where(r == c, 1.0, 0.0).astype(F32)
    tinv = eye + x
    xp = x
    for _ in range(int(math.log2(DN_CHUNK)) - 1):
        xp16 = xp.astype(BF16)
        xp = _dot(xp16, xp16)
        tinv = tinv + _dot(tinv.astype(BF16), xp.astype(BF16))
    rhs = jnp.concatenate([vb, kb * jnp.exp(gc)], axis=1).astype(BF16)
    uw = _dot(tinv.astype(BF16), rhs)
    u_all, w_all = uw[:, :HEAD_DIM], uw[:, HEAD_DIM:]
    qk = _dot_nt(q.astype(BF16), k16) * decay
    qe = q * jnp.exp(gc)

    state = state_ref[...]
    v_new, o_inter = [], []
    for i in range(n_chunk):
        lo, hi = i * DN_CHUNK, (i + 1) * DN_CHUNK
        s16 = state.astype(BF16)
        vn = u_all[lo:hi] - _dot(w_all[lo:hi].astype(BF16), s16)
        o_inter.append(_dot(qe[lo:hi].astype(BF16), s16))
        g_last = gc[hi - 1:hi, :]
        kd = k[lo:hi] * jnp.exp(g_last - gc[lo:hi])
        state = state * jnp.exp(g_last) + _dot_tn(kd.astype(BF16), vn.astype(BF16))
        v_new.append(vn)
    state_ref[...] = state
    v_new = jnp.concatenate(v_new, axis=0)
    o = jnp.concatenate(o_inter, axis=0) + _dot(qk.astype(BF16), v_new.astype(BF16))

    o = o * lax.rsqrt(jnp.mean(o * o, axis=-1, keepdims=True) + RMS_EPS) * na_ref[...]
    o_ref[...] = (o * _silu(z_ref[...].astype(F32))).astype(o_ref.dtype)


def _deltanet(proj, col0, conv_w, gates, gates_t, norm_a, bsz, seq, h_a):
    m = proj.shape[0]
    conv_k = conv_w.shape[0]
    tile = _tile(seq, 256)
    nt = seq // tile
    blk0 = col0 // HEAD_DIM

    def tok(off):
        return pl.BlockSpec((tile, HEAD_DIM), lambda b, h, t: (b * nt + t, blk0 + off * h_a + h))

    def cw(off):
        return pl.BlockSpec((conv_k, HEAD_DIM), lambda b, h, t: (0, off * h_a + h))

    return pl.pallas_call(
        functools.partial(_deltanet_kernel, h_a=h_a, tile=tile, conv_k=conv_k),
        grid=(bsz, h_a, nt),
        in_specs=[tok(0), tok(1), tok(2), tok(3), cw(0), cw(1), cw(2),
                  pl.BlockSpec((tile, LANES), lambda b, h, t: (b * nt + t, 0)),
                  pl.BlockSpec((None, LANES, tile), lambda b, h, t: (b, 0, t)),
                  pl.BlockSpec((1, HEAD_DIM), lambda b, h, t: (0, 0))],
        out_specs=pl.BlockSpec((tile, HEAD_DIM), lambda b, h, t: (b * nt + t, h)),
        out_shape=jax.ShapeDtypeStruct((m, h_a * HEAD_DIM), BF16),
        scratch_shapes=[pltpu.VMEM((HEAD_DIM, HEAD_DIM), F32),
                        pltpu.VMEM((3, SUBLANES, HEAD_DIM), F32)],
        compiler_params=_params(("parallel", "parallel", "arbitrary")),
        name="deltanet",
    )(proj, proj, proj, proj, conv_w, conv_w, conv_w, gates, gates_t, norm_a.reshape(1, HEAD_DIM))


def _gelu(x):
    return 0.5 * x * (1.0 + lax.erf(x * (2.0 ** -0.5)))


def _sgu_kernel(uv_ref, lg_ref, lb_ref, ws_ref, bs_ref, o_ref, *, d_b, chunk, tile):
    groups = d_b // HEAD_DIM
    uv = uv_ref[...].astype(F32)
    u = _gelu(uv[:, :d_b])
    v = _gelu(uv[:, d_b:])
    mu = jnp.mean(v, axis=-1, keepdims=True)
    vc = v - mu
    var = jnp.mean(vc * vc, axis=-1, keepdims=True)
    vn = (vc * lax.rsqrt(var + LN_EPS) * lg_ref[...] + lb_ref[...]).astype(BF16)
    r = lax.broadcasted_iota(jnp.int32, (chunk, chunk), 0)
    c = lax.broadcasted_iota(jnp.int32, (chunk, chunk), 1)
    for g in range(groups):
        w = jnp.where(r >= c, ws_ref[g], 0.0).astype(BF16)
        bias = bs_ref[:, g:g + 1]
        cs = slice(g * HEAD_DIM, (g + 1) * HEAD_DIM)
        for n in range(tile // chunk):
            rs = slice(n * chunk, (n + 1) * chunk)
            mixed = _dot(w, vn[rs, cs]) + bias
            o_ref[rs, cs] = (u[rs, cs] * mixed).astype(o_ref.dtype)


def _sgu(proj, d_b, ln_g, ln_b, w_s, b_s):
    m = proj.shape[0]
    groups, chunk, _ = w_s.shape
    tile = _tile(m, 256)
    return pl.pallas_call(
        functools.partial(_sgu_kernel, d_b=d_b, chunk=chunk, tile=tile),
        grid=(m // tile,),
        in_specs=[pl.BlockSpec((tile, 2 * d_b), lambda i: (i, 0)),
                  pl.BlockSpec((1, d_b), lambda i: (0, 0)),
                  pl.BlockSpec((1, d_b), lambda i: (0, 0)),
                  pl.BlockSpec((groups, chunk, chunk), lambda i: (0, 0, 0)),
                  pl.BlockSpec((chunk, groups), lambda i: (0, 0))],
        out_specs=pl.BlockSpec((tile, d_b), lambda i: (i, 0)),
        out_shape=jax.ShapeDtypeStruct((m, d_b), BF16),
        compiler_params=_params(("parallel",)),
        name="sgu",
    )(proj, ln_g.reshape(1, d_b), ln_b.reshape(1, d_b), w_s, b_s.T)


def _fox_kernel(q_ref, k_ref, v_ref, g_ref, gt_ref, nc_ref, o_ref, m_ref, l_ref, acc_ref, cq_ref,
                *, lane0, tq, tk):
    h = pl.program_id(1)
    qi = pl.program_id(2)
    ki = pl.program_id(3)

    @pl.when(ki == 0)
    def _():
        m_ref[...] = jnp.full_like(m_ref, -jnp.inf)
        l_ref[...] = jnp.zeros_like(l_ref)
        acc_ref[...] = jnp.zeros_like(acc_ref)
        cq_ref[...] = _pick_lane(g_ref[...], lane0 + h)

    last = (qi * tq + tq - 1) // tk

    @pl.when(ki <= last)
    def _():
        s = _dot_nt(q_ref[...], k_ref[...]) * (HEAD_DIM ** -0.5)
        s = s + (cq_ref[...] - gt_ref[pl.ds(lane0 + h, 1), :])
        rows = qi * tq + lax.broadcasted_iota(jnp.int32, (tq, tk), 0)
        cols = ki * tk + lax.broadcasted_iota(jnp.int32, (tq, tk), 1)
        s = jnp.where(cols <= rows, s, -jnp.inf)
        m_prev = m_ref[...]
        m_new = jnp.maximum(m_prev, jnp.max(s, axis=-1, keepdims=True))
        alpha = jnp.exp(m_prev - m_new)
        p = jnp.exp(s - m_new)
        l_ref[...] = alpha * l_ref[...] + jnp.sum(p, axis=-1, keepdims=True)
        acc_ref[...] = alpha * acc_ref[...] + _dot(p.astype(BF16), v_ref[...])
        m_ref[...] = m_new

    @pl.when(ki == last)
    def _():
        o = acc_ref[...] / l_ref[...]
        o = o * lax.rsqrt(jnp.mean(o * o, axis=-1, keepdims=True) + RMS_EPS) * nc_ref[...]
        o_ref[...] = o.astype(o_ref.dtype)


def _fox(proj, col0, gates, gates_t, norm_c, bsz, seq, h_c, lane0):
    m = proj.shape[0]
    tq = tk = _tile(seq, 1024)
    nq, nk = seq // tq, seq // tk
    blk0 = col0 // HEAD_DIM

    def kv_row(b, qi, ki):
        return b * nk + jnp.minimum(ki, (qi * tq + tq - 1) // tk)

    return pl.pallas_call(
        functools.partial(_fox_kernel, lane0=lane0, tq=tq, tk=tk),
        grid=(bsz, h_c, nq, nk),
        in_specs=[pl.BlockSpec((tq, HEAD_DIM), lambda b, h, qi, ki: (b * nq + qi, blk0 + h)),
                  pl.BlockSpec((tk, HEAD_DIM), lambda b, h, qi, ki: (kv_row(b, qi, ki), blk0 + h_c + h)),
                  pl.BlockSpec((tk, HEAD_DIM), lambda b, h, qi, ki: (kv_row(b, qi, ki), blk0 + 2 * h_c + h)),
                  pl.BlockSpec((tq, LANES), lambda b, h, qi, ki: (b * nq + qi, 0)),
                  pl.BlockSpec((None, LANES, tk),
                               lambda b, h, qi, ki: (b, 0, jnp.minimum(ki, (qi * tq + tq - 1) // tk))),
                  pl.BlockSpec((1, HEAD_DIM), lambda b, h, qi, ki: (0, 0))],
        out_specs=pl.BlockSpec((tq, HEAD_DIM), lambda b, h, qi, ki: (b * nq + qi, h)),
        out_shape=jax.ShapeDtypeStruct((m, h_c * HEAD_DIM), BF16),
        scratch_shapes=[pltpu.VMEM((tq, 1), F32), pltpu.VMEM((tq, 1), F32),
                        pltpu.VMEM((tq, HEAD_DIM), F32), pltpu.VMEM((tq, 1), F32)],
        compiler_params=_params(("parallel", "parallel", "parallel", "arbitrary")),
        name="fox_attention",
    )(proj, proj, proj, gates, gates_t, norm_c.reshape(1, HEAD_DIM))


def kernel(x, c, w_ada, b_ada, ln_g, ln_b, w_ffn_in, w_ffn_out, w_in, conv_w, a_log, dt_bias, norm_a,
           sgu_ln_g, sgu_ln_b, w_s, b_s, b_f, norm_c, w_o):
    bsz, seq, d = x.shape
    depth = w_ada.shape[0]
    m = bsz * seq
    h_a, h_c = a_log.shape[1], b_f.shape[1]
    d_a, d_c = h_a * HEAD_DIM, h_c * HEAD_DIM
    d_b = sgu_ln_g.shape[1]
    alpha = (2.0 * depth) ** 0.25
    assert 2 * h_a + h_c <= LANES and seq % DN_CHUNK == 0 and seq % w_s.shape[-1] == 0

    c_pad = jnp.zeros((SUBLANES, d), F32).at[:bsz].set(c)
    mod = _ada(c_pad, w_ada, b_ada)[:, :bsz]

    def mods(l):
        return [v.reshape(bsz, 1, d) for v in jnp.split(mod[l], 9, axis=-1)]

    o_qkv_a = 0
    o_beta = o_qkv_a + 4 * d_a
    o_uv = o_beta + 2 * h_a
    o_qkv_c = o_uv + 2 * d_b
    o_f = o_qkv_c + 3 * d_c
    col_a = 2 * d_b
    col_c = col_a + 4 * d_a

    x2 = x.reshape(m, d)
    h = None
    for l in range(depth):
        sh1, sc1, ga1, sh2, sc2, ga2, sh3, sc3, ga3 = mods(l)
        if l == 0:
            h = _modulate(x2, sh1, sc1, seq)
        wl = w_in[l]
        w_big = jnp.concatenate([wl[:, o_uv:o_uv + 2 * d_b], wl[:, o_qkv_a:o_qkv_a + 4 * d_a],
                                 wl[:, o_qkv_c:o_qkv_c + 3 * d_c]], axis=1).astype(BF16)
        w_small = jnp.concatenate([wl[:, o_beta:o_beta + 2 * h_a], wl[:, o_f:o_f + h_c],
                                   jnp.zeros((d, LANES - 2 * h_a - h_c), F32)], axis=1).astype(BF16)

        act = _swiglu_in(h, w_ffn_in[l, 0].astype(BF16))
        y = _matmul_ktiled(act, w_ffn_out[l, 0].astype(BF16), BF16, name="ffn_out")
        x2, h = _resid_ln(x2, y, ga1, ln_g[l, 0], ln_b[l, 0], (sh2, sc2), seq, alpha, 0.5)

        proj = _matmul(h, w_big, BF16, tn=512, name="mixer_in")
        small = _matmul(h, w_small, F32, tn=LANES, name="mixer_in_small")
        gates, gates_t = _gates(small, a_log[l], dt_bias[l], b_f[l], bsz, seq)
        o_a = _deltanet(proj, col_a, conv_w[l], gates, gates_t, norm_a[l], bsz, seq, h_a)
        o_b = _sgu(proj, d_b, sgu_ln_g[l], sgu_ln_b[l], w_s[l], b_s[l])
        o_c = _fox(proj, col_c, gates, gates_t, norm_c[l], bsz, seq, h_c, 2 * h_a)
        mix = jnp.concatenate([o_a, o_b, o_c], axis=1)
        y = _matmul(mix, w_o[l].astype(BF16), BF16, name="mixer_out")
        x2, h = _resid_ln(x2, y, ga2, ln_g[l, 1], ln_b[l, 1], (sh3, sc3), seq, alpha, 1.0)

        act = _swiglu_in(h, w_ffn_in[l, 1].astype(BF16))
        y = _matmul_ktiled(act, w_ffn_out[l, 1].astype(BF16), BF16, name="ffn_out")
        nxt = None
        if l + 1 < depth:
            nsh, nsc = mods(l + 1)[:2]
            nxt = (nsh, nsc)
        x2, h = _resid_ln(x2, y, ga3, ln_g[l, 2], ln_b[l, 2], nxt, seq, alpha, 0.5)
    return x2.reshape(bsz, seq, d)
```

```python
import functools
import math

import jax
import jax.numpy as jnp
from jax import lax
from jax.experimental import pallas as pl
from jax.experimental.pallas import tpu as pltpu

F32 = jnp.float32
BF16 = jnp.bfloat16

HEAD_DIM = 128
DN_CHUNK = 64
LN_EPS = 1e-5
RMS_EPS = 1e-6
LOG2E = 1.4426950408889634
LANES = 128
SUBLANES = 8
VMEM_LIMIT = 56 * 1024 * 1024


def _params(sem, vmem=VMEM_LIMIT):
    return pltpu.CompilerParams(dimension_semantics=sem, vmem_limit_bytes=vmem)


def _tile(n, pref):
    t = min(n, pref)
    while n % t:
        t //= 2
    return t


def _silu(x):
    return x * jax.nn.sigmoid(x)


def _softplus(x):
    return jnp.maximum(x, 0.0) + jnp.log1p(jnp.exp(-jnp.abs(x)))


def _dot(a, b):
    return jnp.dot(a, b, preferred_element_type=F32)


def _dot_nt(a, b):
    return lax.dot_general(a, b, (((1,), (1,)), ((), ())), preferred_element_type=F32)


def _dot_tn(a, b):
    return lax.dot_general(a, b, (((0,), (0,)), ((), ())), preferred_element_type=F32)


def _pick_lane(x, idx):
    lane = lax.broadcasted_iota(jnp.int32, x.shape, 1)
    return jnp.sum(jnp.where(lane == idx, x, 0.0), axis=-1, keepdims=True)


def _ada_kernel(c_ref, w_ref, b_ref, o_ref):
    c = c_ref[...]
    o_ref[...] = _dot(_silu(c).astype(BF16), w_ref[...].astype(BF16)) + b_ref[...]


def _ada(c_pad, w_ada, b_ada):
    depth, d, n = w_ada.shape
    bp = c_pad.shape[0]
    tn = _tile(n, 512)
    return pl.pallas_call(
        _ada_kernel,
        grid=(depth, n // tn),
        in_specs=[
            pl.BlockSpec((bp, d), lambda l, j: (0, 0)),
            pl.BlockSpec((None, d, tn), lambda l, j: (l, 0, j)),
            pl.BlockSpec((None, 1, tn), lambda l, j: (l, 0, j)),
        ],
        out_specs=pl.BlockSpec((None, bp, tn), lambda l, j: (l, 0, j)),
        out_shape=jax.ShapeDtypeStruct((depth, bp, n), F32),
        compiler_params=_params(("parallel", "parallel")),
        name="ada_mod",
    )(c_pad, w_ada, b_ada.reshape(depth, 1, n))


def _modulate_kernel(x_ref, sh_ref, sc_ref, h_ref):
    h_ref[...] = (x_ref[...] * (1.0 + sc_ref[...]) + sh_ref[...]).astype(h_ref.dtype)


def _modulate(x2, sh, sc, seq):
    m, d = x2.shape
    tm = _tile(seq, 512)
    per = seq // tm
    vec = pl.BlockSpec((None, 1, d), lambda i: (i // per, 0, 0))
    return pl.pallas_call(
        _modulate_kernel,
        grid=(m // tm,),
        in_specs=[pl.BlockSpec((tm, d), lambda i: (i, 0)), vec, vec],
        out_specs=pl.BlockSpec((tm, d), lambda i: (i, 0)),
        out_shape=jax.ShapeDtypeStruct((m, d), BF16),
        compiler_params=_params(("parallel",)),
        name="modulate",
    )(x2, sh, sc)


def _win_big_kernel(lo_ref, hi_ref, o_ref, *, n_aligned, group, keep):
    j = pl.program_id(1)

    @pl.when(j < n_aligned)
    def _():
        o_ref[...] = lo_ref[...].astype(o_ref.dtype)

    @pl.when(j >= n_aligned)
    def _():
        lane = lax.broadcasted_iota(jnp.int32, (lo_ref.shape[0], LANES), 1)
        for b in range(group):
            lo = lo_ref[:, b * LANES:(b + 1) * LANES]
            hi = lo_ref[:, (b + 1) * LANES:(b + 2) * LANES] if b + 1 < group else hi_ref[...]
            shifted = jnp.where(lane < keep, pltpu.roll(lo, keep, 1), pltpu.roll(hi, keep, 1))
            o_ref[:, b * LANES:(b + 1) * LANES] = shifted.astype(o_ref.dtype)


def _win_small_kernel(a_ref, b_ref, o_ref, *, n0, n1):
    lane = lax.broadcasted_iota(jnp.int32, a_ref.shape, 1)
    o_ref[...] = jnp.where(lane < n0, a_ref[...], jnp.where(lane < n0 + n1, b_ref[...], 0.0)).astype(o_ref.dtype)


def _win_prep(w_in, d_a, d_b, d_c, h_a, h_c):
    depth, d, _ = w_in.shape
    n_al = 4 * d_a // LANES
    n_sh = (2 * d_b + 3 * d_c) // LANES
    shift = 2 * h_a
    group = math.gcd(math.gcd(n_al, n_sh), 4)
    cw = group * LANES
    big = pl.pallas_call(
        functools.partial(_win_big_kernel, n_aligned=n_al // group, group=group, keep=LANES - shift),
        grid=(depth, (n_al + n_sh) // group),
        in_specs=[pl.BlockSpec((None, d, cw), lambda l, j: (l, 0, j)),
                  pl.BlockSpec((None, d, LANES), lambda l, j: (l, 0, (j + 1) * group))],
        out_specs=pl.BlockSpec((None, d, cw), lambda l, j: (l, 0, j)),
        out_shape=jax.ShapeDtypeStruct((depth, d, (n_al + n_sh) * LANES), BF16),
        compiler_params=_params(("parallel", "parallel")),
        name="w_in_relayout",
    )(w_in, w_in)
    small = pl.pallas_call(
        functools.partial(_win_small_kernel, n0=shift, n1=h_c),
        grid=(depth,),
        in_specs=[pl.BlockSpec((None, d, LANES), lambda l: (l, 0, n_al)),
                  pl.BlockSpec((None, d, LANES), lambda l: (l, 0, n_al + n_sh))],
        out_specs=pl.BlockSpec((None, d, LANES), lambda l: (l, 0, 0)),
        out_shape=jax.ShapeDtypeStruct((depth, d, LANES), BF16),
        compiler_params=_params(("parallel",)),
        name="w_in_small",
    )(w_in, w_in)
    return big, small


def _mm_kernel(a_ref, w_ref, o_ref):
    o_ref[...] = _dot(a_ref[...], w_ref[...]).astype(o_ref.dtype)


def _wspec(lead, rows, cols, index):
    return pl.BlockSpec((None,) * len(lead) + (rows, cols), lambda *g: tuple(lead) + tuple(index(*g)))


def _matmul(a, w, lead, out_dtype, tm=1024, tn=1024, name="matmul"):
    m, k = a.shape
    n = w.shape[-1]
    tm, tn = _tile(m, tm), _tile(n, tn)
    return pl.pallas_call(
        _mm_kernel,
        grid=(m // tm, n // tn),
        in_specs=[pl.BlockSpec((tm, k), lambda i, j: (i, 0)),
                  _wspec(lead, k, tn, lambda i, j: (0, j))],
        out_specs=pl.BlockSpec((tm, tn), lambda i, j: (i, j)),
        out_shape=jax.ShapeDtypeStruct((m, n), out_dtype),
        compiler_params=_params(("parallel", "parallel")),
        name=name,
    )(a, w)


def _swiglu_kernel(a_ref, wg_ref, wu_ref, o_ref):
    a = a_ref[...]
    g = _dot(a, wg_ref[...])
    u = _dot(a, wu_ref[...])
    o_ref[...] = (_silu(g) * u).astype(o_ref.dtype)


def _swiglu_in(h, w_in, lead, tm=1024, tf=512):
    m, k = h.shape
    f = w_in.shape[-1] // 2
    tm, tf = _tile(m, tm), _tile(f, tf)
    nf = f // tf
    return pl.pallas_call(
        _swiglu_kernel,
        grid=(m // tm, nf),
        in_specs=[pl.BlockSpec((tm, k), lambda i, j: (i, 0)),
                  _wspec(lead, k, tf, lambda i, j: (0, j)),
                  _wspec(lead, k, tf, lambda i, j: (0, j + nf))],
        out_specs=pl.BlockSpec((tm, tf), lambda i, j: (i, j)),
        out_shape=jax.ShapeDtypeStruct((m, f), BF16),
        compiler_params=_params(("parallel", "parallel")),
        name="ffn_in_swiglu",
    )(h, w_in, w_in)


def _mm_acc_kernel(a_ref, w_ref, o_ref, acc_ref):
    kk = pl.program_id(2)

    @pl.when(kk == 0)
    def _():
        acc_ref[...] = jnp.zeros_like(acc_ref)

    acc_ref[...] += _dot(a_ref[...], w_ref[...])

    @pl.when(kk == pl.num_programs(2) - 1)
    def _():
        o_ref[...] = acc_ref[...].astype(o_ref.dtype)


def _matmul_ktiled(a, w, lead, out_dtype, tm=1024, tn=1024, tk=2048, name="matmul_k"):
    m, k = a.shape
    n = w.shape[-1]
    tm, tn, tk = _tile(m, tm), _tile(n, tn), _tile(k, tk)
    return pl.pallas_call(
        _mm_acc_kernel,
        grid=(m // tm, n // tn, k // tk),
        in_specs=[pl.BlockSpec((tm, tk), lambda i, j, q: (i, q)),
                  _wspec(lead, tk, tn, lambda i, j, q: (q, j))],
        out_specs=pl.BlockSpec((tm, tn), lambda i, j, q: (i, j)),
        out_shape=jax.ShapeDtypeStruct((m, n), out_dtype),
        scratch_shapes=[pltpu.VMEM((tm, tn), F32)],
        compiler_params=_params(("parallel", "parallel", "arbitrary")),
        name=name,
    )(a, w)


def _resid_ln_kernel(x_ref, y_ref, ga_ref, g_ref, b_ref, *rest, alpha, coef, with_next):
    if with_next:
        sh_ref, sc_ref, xo_ref, h_ref = rest
    else:
        (xo_ref,) = rest
    r = alpha * x_ref[...] + (coef * ga_ref[...]) * y_ref[...].astype(F32)
    mu = jnp.mean(r, axis=-1, keepdims=True)
    rc = r - mu
    var = jnp.mean(rc * rc, axis=-1, keepdims=True)
    xn = rc * lax.rsqrt(var + LN_EPS) * g_ref[...] + b_ref[...]
    xo_ref[...] = xn
    if with_next:
        h_ref[...] = (xn * (1.0 + sc_ref[...]) + sh_ref[...]).astype(h_ref.dtype)


def _resid_ln(x2, y, ga, ln_g, ln_b, nxt, seq, alpha, coef):
    m, d = x2.shape
    tm = _tile(seq, 256)
    per = seq // tm
    row = pl.BlockSpec((tm, d), lambda i: (i, 0))
    vec = pl.BlockSpec((None, 1, d), lambda i: (i // per, 0, 0))
    par = pl.BlockSpec((1, d), lambda i: (0, 0))
    with_next = nxt is not None
    in_specs = [row, row, vec, par, par]
    args = [x2, y, ga, ln_g.reshape(1, d), ln_b.reshape(1, d)]
    out_specs = [row]
    out_shape = [jax.ShapeDtypeStruct((m, d), F32)]
    if with_next:
        in_specs += [vec, vec]
        args += list(nxt)
        out_specs.append(row)
        out_shape.append(jax.ShapeDtypeStruct((m, d), BF16))
    res = pl.pallas_call(
        functools.partial(_resid_ln_kernel, alpha=alpha, coef=coef, with_next=with_next),
        grid=(m // tm,),
        in_specs=in_specs,
        out_specs=out_specs,
        out_shape=out_shape,
        compiler_params=_params(("parallel",)),
        name="resid_ln",
    )(*args)
    return (res[0], res[1]) if with_next else (res[0], None)


def _gates_kernel(s_ref, p_ref, g_ref, gt_ref, carry_ref, *, h_a, h_c, tile):
    t = pl.program_id(1)

    @pl.when(t == 0)
    def _():
        carry_ref[...] = jnp.zeros_like(carry_ref)

    x = s_ref[...]
    a_log = p_ref[0:1, :]
    dt_b = p_ref[1:2, :]
    b_f = p_ref[2:3, :]
    lane = lax.broadcasted_iota(jnp.int32, x.shape, 1)
    beta = jax.nn.sigmoid(x)
    g = -jnp.exp(a_log) * _softplus(x + dt_b)
    lf = -_softplus(-(x + b_f))
    is_g = (lane >= h_a) & (lane < 2 * h_a)
    is_f = (lane >= 2 * h_a) & (lane < 2 * h_a + h_c)
    g = jnp.where(is_g, g, 0.0)
    lf = jnp.where(is_f, lf, 0.0)
    r = lax.broadcasted_iota(jnp.int32, (tile, tile), 0)
    c = lax.broadcasted_iota(jnp.int32, (tile, tile), 1)
    tri = r >= c
    tril = jnp.where(tri, 1.0, 0.0).astype(F32)
    btril = jnp.where(tri & ((r // DN_CHUNK) == (c // DN_CHUNK)), 1.0, 0.0).astype(F32)
    hi = lax.Precision.HIGHEST
    gcum = jnp.dot(btril, g, precision=hi, preferred_element_type=F32)
    fcum = jnp.dot(tril, lf, precision=hi, preferred_element_type=F32) + carry_ref[0:1, :]
    carry_ref[0:1, :] = fcum[tile - 1:tile, :]
    out = jnp.where(lane < h_a, beta, jnp.where(is_g, gcum, jnp.where(is_f, fcum, 0.0)))
    g_ref[...] = out
    gt_ref[...] = out.T


def _gates(small, a_log, dt_bias, b_f, bsz, seq):
    m = small.shape[0]
    h_a, h_c = a_log.shape[0], b_f.shape[0]
    tile = _tile(seq, 512)
    nt = seq // tile
    p = jnp.zeros((SUBLANES, LANES), F32)
    p = p.at[0, h_a:2 * h_a].set(a_log).at[1, h_a:2 * h_a].set(dt_bias).at[2, 2 * h_a:2 * h_a + h_c].set(b_f)
    return pl.pallas_call(
        functools.partial(_gates_kernel, h_a=h_a, h_c=h_c, tile=tile),
        grid=(bsz, nt),
        in_specs=[pl.BlockSpec((tile, LANES), lambda b, t: (b * nt + t, 0)),
                  pl.BlockSpec((SUBLANES, LANES), lambda b, t: (0, 0))],
        out_specs=[pl.BlockSpec((tile, LANES), lambda b, t: (b * nt + t, 0)),
                   pl.BlockSpec((None, LANES, tile), lambda b, t: (b, 0, t))],
        out_shape=[jax.ShapeDtypeStruct((m, LANES), F32),
                   jax.ShapeDtypeStruct((bsz, LANES, seq), F32)],
        scratch_shapes=[pltpu.VMEM((SUBLANES, LANES), F32)],
        compiler_params=_params(("arbitrary", "arbitrary")),
        name="gates",
    )(small, p)


def _split2(x):
    hi = x.astype(BF16)
    return hi, (x - hi.astype(F32)).astype(BF16)


def _delta_heads(q, k, v, beta, gc, gr, state, masks):
    same, tri, strict, eye = masks
    heads = range(len(q))
    tile = q[0].shape[0]
    decay = [jnp.where(tri, jnp.exp(jnp.where(tri, gc[h] - gr[h], 0.0)), 0.0) for h in heads]
    kb = [k[h] * beta[h] for h in heads]
    k16 = [k[h].astype(BF16) for h in heads]
    kk = [_dot_nt(kb[h].astype(BF16), k16[h]) * decay[h] for h in heads]
    lmat = [jnp.where(strict, kk[h], 0.0) for h in heads]
    xp = [-lmat[h] for h in heads]
    tinv = [eye + xp[h] for h in heads]
    for _ in range(int(math.log2(DN_CHUNK)) - 1):
        xp16 = [xp[h].astype(BF16) for h in heads]
        xp = [_dot(xp16[h], xp16[h]) for h in heads]
        tinv = [tinv[h] + _dot(tinv[h].astype(BF16), xp[h].astype(BF16)) for h in heads]
    tinv16 = [tinv[h].astype(BF16) for h in heads]
    egc = [jnp.exp(gc[h]) for h in heads]
    rhs = [jnp.concatenate([v[h] * beta[h], kb[h] * egc[h]], axis=1) for h in heads]
    uw = [_dot(tinv16[h], rhs[h].astype(BF16)) for h in heads]
    l_hi, l_lo = zip(*[_split2(lmat[h]) for h in heads])
    u_hi, u_lo = zip(*[_split2(uw[h]) for h in heads])
    lu = [_dot(l_hi[h], u_hi[h]) + _dot(l_hi[h], u_lo[h]) + _dot(l_lo[h], u_hi[h]) for h in heads]
    res = [rhs[h] - uw[h] - lu[h] for h in heads]
    uw = [uw[h] + _dot(tinv16[h], res[h].astype(BF16)) for h in heads]
    qk = [_dot_nt(q[h].astype(BF16), k16[h]) * decay[h] for h in heads]
    qe = [q[h] * egc[h] for h in heads]

    state = list(state)
    v_new = [[] for _ in heads]
    o_inter = [[] for _ in heads]
    for i in range(tile // DN_CHUNK):
        lo, hi = i * DN_CHUNK, (i + 1) * DN_CHUNK
        s16 = [state[h].astype(BF16) for h in heads]
        vn = [uw[h][lo:hi, :HEAD_DIM] - _dot(uw[h][lo:hi, HEAD_DIM:].astype(BF16), s16[h]) for h in heads]
        for h in heads:
            o_inter[h].append(_dot(qe[h][lo:hi].astype(BF16), s16[h]))
            v_new[h].append(vn[h])
        g_last = [gc[h][hi - 1:hi, :] for h in heads]
        kd = [k[h][lo:hi] * jnp.exp(g_last[h] - gc[h][lo:hi]) for h in heads]
        state = [state[h] * jnp.exp(g_last[h]) + _dot_tn(kd[h].astype(BF16), vn[h].astype(BF16)) for h in heads]
    o = [jnp.concatenate(o_inter[h], axis=0)
         + _dot(qk[h].astype(BF16), jnp.concatenate(v_new[h], axis=0).astype(BF16)) for h in heads]
    return o, state


def _deltanet_kernel(q_ref, k_ref, v_ref, z_ref, wq_ref, wk_ref, wv_ref, g_ref, gt_ref, na_ref, mix_ref,
                     o_ref, state_ref, tail_ref, *, h_a, hpg, tile, conv_k):
    del mix_ref
    hg = pl.program_id(1)
    t = pl.program_id(2)

    @pl.when(t == 0)
    def _():
        state_ref[...] = jnp.zeros_like(state_ref)
        tail_ref[...] = jnp.zeros_like(tail_ref)

    width = hpg * HEAD_DIM
    row8 = lax.broadcasted_iota(jnp.int32, (SUBLANES, width), 0)

    def conv_silu(x_ref, w_ref, slot):
        x = x_ref[...].astype(F32)
        w = w_ref[...]
        prev = tail_ref[slot]
        acc = x * w[conv_k - 1:conv_k, :]
        for sh in range(1, conv_k):
            xs = pltpu.roll(x, sh, 0)
            head = jnp.where(row8 < sh, pltpu.roll(prev, sh, 0), xs[:SUBLANES])
            xs = jnp.concatenate([head, xs[SUBLANES:]], axis=0)
            acc = acc + xs * w[conv_k - 1 - sh:conv_k - sh, :]
        tail_ref[slot] = x[tile - SUBLANES:, :]
        return _silu(acc)

    def l2n(x):
        return x * lax.rsqrt(jnp.sum(x * x, axis=-1, keepdims=True) + RMS_EPS)

    q_all = conv_silu(q_ref, wq_ref, 0)
    k_all = conv_silu(k_ref, wk_ref, 1)
    v_all = conv_silu(v_ref, wv_ref, 2)
    gates = g_ref[...]

    r = lax.broadcasted_iota(jnp.int32, (tile, tile), 0)
    c = lax.broadcasted_iota(jnp.int32, (tile, tile), 1)
    same = (r // DN_CHUNK) == (c // DN_CHUNK)
    masks = (same, same & (r >= c), same & (r > c), jnp.where(r == c, 1.0, 0.0).astype(F32))

    heads = range(hpg)
    cs = [slice(hh * HEAD_DIM, (hh + 1) * HEAD_DIM) for hh in heads]
    q = [l2n(q_all[:, cs[hh]]) * (HEAD_DIM ** -0.5) for hh in heads]
    k = [l2n(k_all[:, cs[hh]]) for hh in heads]
    v = [v_all[:, cs[hh]] for hh in heads]
    beta = [_pick_lane(gates, hg * hpg + hh) for hh in heads]
    gc = [_pick_lane(gates, h_a + hg * hpg + hh) for hh in heads]
    gr = [gt_ref[pl.ds(h_a + hg * hpg + hh, 1), :] for hh in heads]
    gate = _silu(z_ref[...].astype(F32))
    o, state = _delta_heads(q, k, v, beta, gc, gr, [state_ref[hh] for hh in heads], masks)
    for hh in heads:
        state_ref[hh] = state[hh]
        oh = o[hh] * lax.rsqrt(jnp.mean(o[hh] * o[hh], axis=-1, keepdims=True) + RMS_EPS) * na_ref[...]
        o_ref[:, cs[hh]] = (oh * gate[:, cs[hh]]).astype(o_ref.dtype)


def _deltanet(proj, mix, conv_w, gates, gates_t, norm_a, bsz, seq, h_a):
    conv_k = conv_w.shape[0]
    tile = _tile(seq, 256)
    nt = seq // tile
    hpg = next(g for g in (4, 3, 2, 1) if h_a % g == 0)
    ng = h_a // hpg
    width = hpg * HEAD_DIM

    def tok(off):
        return pl.BlockSpec((tile, width), lambda b, g, t: (b * nt + t, off * ng + g))

    def cw(off):
        return pl.BlockSpec((conv_k, width), lambda b, g, t: (0, off * ng + g))

    return pl.pallas_call(
        functools.partial(_deltanet_kernel, h_a=h_a, hpg=hpg, tile=tile, conv_k=conv_k),
        grid=(bsz, ng, nt),
        in_specs=[tok(0), tok(1), tok(2), tok(3), cw(0), cw(1), cw(2),
                  pl.BlockSpec((tile, LANES), lambda b, g, t: (b * nt + t, 0)),
                  pl.BlockSpec((None, LANES, tile), lambda b, g, t: (b, 0, t)),
                  pl.BlockSpec((1, HEAD_DIM), lambda b, g, t: (0, 0)),
                  pl.BlockSpec(memory_space=pl.ANY)],
        out_specs=pl.BlockSpec((tile, width), lambda b, g, t: (b * nt + t, g)),
        out_shape=jax.ShapeDtypeStruct(mix.shape, mix.dtype),
        scratch_shapes=[pltpu.VMEM((hpg, HEAD_DIM, HEAD_DIM), F32),
                        pltpu.VMEM((3, SUBLANES, width), F32)],
        input_output_aliases={10: 0},
        compiler_params=_params(("parallel", "parallel", "arbitrary")),
        name="deltanet",
    )(proj, proj, proj, proj, conv_w, conv_w, conv_w, gates, gates_t, norm_a.reshape(1, HEAD_DIM), mix)


def _gelu(x):
    return 0.5 * x * (1.0 + lax.erf(x * (2.0 ** -0.5)))


def _sgu_kernel(u_ref, vb_ref, v_ref, lg_ref, lb_ref, ws_ref, bs_ref, mix_ref, o_ref, *, chunk, tile, gpb):
    del mix_ref
    jb = pl.program_id(1)
    v = _gelu(v_ref[...].astype(F32))
    mu = jnp.mean(v, axis=-1, keepdims=True)
    vc = v - mu
    rstd = lax.rsqrt(jnp.mean(vc * vc, axis=-1, keepdims=True) + LN_EPS)
    u = _gelu(u_ref[...].astype(F32))
    vn = ((_gelu(vb_ref[...].astype(F32)) - mu) * rstd * lg_ref[...] + lb_ref[...]).astype(BF16)
    r = lax.broadcasted_iota(jnp.int32, (chunk, chunk), 0)
    c = lax.broadcasted_iota(jnp.int32, (chunk, chunk), 1)
    bs = bs_ref[...]
    for gl in range(gpb):
        g = jb * gpb + gl
        w = jnp.where(r >= c, ws_ref[g], 0.0).astype(BF16)
        bias = _pick_lane(bs, g)
        cs = slice(gl * HEAD_DIM, (gl + 1) * HEAD_DIM)
        for n in range(tile // chunk):
            rs = slice(n * chunk, (n + 1) * chunk)
            mixed = _dot(w, vn[rs, cs]) + bias
            o_ref[rs, cs] = (u[rs, cs] * mixed).astype(o_ref.dtype)


def _sgu(proj, mix, col_uv, col_out, d_b, ln_g, ln_b, w_s, b_s):
    m = proj.shape[0]
    groups, chunk, _ = w_s.shape
    tile = _tile(m, 256)
    ob = math.gcd(col_out, d_b)
    assert col_uv % d_b == 0 and ob % HEAD_DIM == 0
    nb = d_b // ob
    return pl.pallas_call(
        functools.partial(_sgu_kernel, chunk=chunk, tile=tile, gpb=ob // HEAD_DIM),
        grid=(m // tile, nb),
        in_specs=[pl.BlockSpec((tile, ob), lambda i, j: (i, col_uv // ob + j)),
                  pl.BlockSpec((tile, ob), lambda i, j: (i, (col_uv + d_b) // ob + j)),
                  pl.BlockSpec((tile, d_b), lambda i, j: (i, col_uv // d_b + 1)),
                  pl.BlockSpec((1, ob), lambda i, j: (0, j)),
                  pl.BlockSpec((1, ob), lambda i, j: (0, j)),
                  pl.BlockSpec((groups, chunk, chunk), lambda i, j: (0, 0, 0)),
                  pl.BlockSpec((chunk, groups), lambda i, j: (0, 0)),
                  pl.BlockSpec(memory_space=pl.ANY)],
        out_specs=pl.BlockSpec((tile, ob), lambda i, j: (i, col_out // ob + j)),
        out_shape=jax.ShapeDtypeStruct(mix.shape, mix.dtype),
        input_output_aliases={7: 0},
        compiler_params=_params(("parallel", "parallel")),
        name="sgu",
    )(proj, proj, proj, ln_g.reshape(1, d_b), ln_b.reshape(1, d_b), w_s, b_s.T, mix)


def _split3(x):
    hi = x.astype(BF16).astype(F32)
    r1 = x - hi
    mid = r1.astype(BF16).astype(F32)
    lo = (r1 - mid).astype(BF16).astype(F32)
    return hi, mid, lo


def _fox_kernel(q_ref, k_ref, v_ref, g_ref, nc_ref, mix_ref, o_ref, kaug_ref, vaug_ref, qaug_ref, m_ref, acc_ref,
                *, lane0, tq, tk, seq, prep):
    del mix_ref
    h = pl.program_id(1)
    qi = pl.program_id(2)

    @pl.when(qi == 0)
    def _():
        lane = lax.broadcasted_iota(jnp.int32, (prep, LANES), 1)
        ones0 = jnp.where(lane == 0, 1.0, 0.0).astype(BF16)
        for r0 in range(0, seq, prep):
            rows = slice(r0, r0 + prep)
            hi, mid, lo = _split3(_pick_lane(g_ref[rows, :], lane0 + h) * LOG2E)
            ex = jnp.where(lane < 3, 1.0,
                           jnp.where(lane == 3, -hi, jnp.where(lane == 4, -mid, jnp.where(lane == 5, -lo, 0.0))))
            kaug_ref[rows, :HEAD_DIM] = k_ref[rows, :]
            kaug_ref[rows, HEAD_DIM:] = ex.astype(BF16)
            vaug_ref[rows, :HEAD_DIM] = v_ref[rows, :]
            vaug_ref[rows, HEAD_DIM:] = ones0

    q0 = pl.multiple_of(qi * tq, tq)
    lane = lax.broadcasted_iota(jnp.int32, (tq, LANES), 1)
    hi, mid, lo = _split3(_pick_lane(g_ref[pl.ds(q0, tq), :], lane0 + h) * LOG2E)
    ex = jnp.where(lane == 0, hi, jnp.where(lane == 1, mid, jnp.where(lane == 2, lo, jnp.where(lane < 6, 1.0, 0.0))))
    qaug_ref[:, :HEAD_DIM] = (q_ref[...].astype(F32) * (HEAD_DIM ** -0.5 * LOG2E)).astype(BF16)
    qaug_ref[:, HEAD_DIM:] = ex.astype(BF16)
    m_ref[...] = jnp.full_like(m_ref, -jnp.inf)
    acc_ref[...] = jnp.zeros_like(acc_ref)

    nsub = tq // tk
    tri = lax.broadcasted_iota(jnp.int32, (tk, tk), 1) <= lax.broadcasted_iota(jnp.int32, (tk, tk), 0)

    def process(subs):
        rows = lambda r: slice(r * tk, (r + 1) * tk)
        scores = [_dot_nt(qaug_ref[rows(r), :], kaug_ref[pl.ds(start, tk), :]) for r, start, _ in subs]
        chains = sorted({r for r, _, _ in subs})
        m = {r: m_ref[rows(r), :] for r in chains}
        acc = {r: acc_ref[rows(r), :] for r in chains}
        for (r, start, diag), s in zip(subs, scores):
            if diag:
                s = jnp.where(tri, s, -jnp.inf)
            m_new = jnp.maximum(m[r], jnp.max(s, axis=-1, keepdims=True))
            p = jnp.exp2(s - m_new)
            acc[r] = jnp.exp2(m[r] - m_new) * acc[r] + _dot(p.astype(BF16), vaug_ref[pl.ds(start, tk), :])
            m[r] = m_new
        for r in chains:
            m_ref[rows(r), :] = m[r]
            acc_ref[rows(r), :] = acc[r]

    def full_square(j, carry):
        base = pl.multiple_of(j * tq, tq)
        process([(r, base + c * tk, False) for c in range(nsub) for r in range(nsub)])
        return carry

    lax.fori_loop(0, qi, full_square, 0)
    process([(r, q0 + d * tk, r == d) for d in range(nsub) for r in range(d, nsub)])

    acc = acc_ref[...]
    o = acc[:, :HEAD_DIM] / acc[:, HEAD_DIM:HEAD_DIM + 1]
    o = o * lax.rsqrt(jnp.mean(o * o, axis=-1, keepdims=True) + RMS_EPS) * nc_ref[...]
    o_ref[...] = o.astype(o_ref.dtype)


def _fox(proj, mix, col_in, col_out, gates, norm_c, bsz, seq, h_c, lane0):
    tq = _tile(seq, 1024)
    tk = _tile(seq, 512)
    nq = seq // tq
    blk_in, blk_out = col_in // HEAD_DIM, col_out // HEAD_DIM
    prep = _tile(seq, 1024)
    return pl.pallas_call(
        functools.partial(_fox_kernel, lane0=lane0, tq=tq, tk=tk, seq=seq, prep=prep),
        grid=(bsz, h_c, nq),
        in_specs=[pl.BlockSpec((tq, HEAD_DIM), lambda b, h, qi: (b * nq + qi, blk_in + h)),
                  pl.BlockSpec((seq, HEAD_DIM), lambda b, h, qi: (b, blk_in + h_c + h)),
                  pl.BlockSpec((seq, HEAD_DIM), lambda b, h, qi: (b, blk_in + 2 * h_c + h)),
                  pl.BlockSpec((seq, LANES), lambda b, h, qi: (b, 0)),
                  pl.BlockSpec((1, HEAD_DIM), lambda b, h, qi: (0, 0)),
                  pl.BlockSpec(memory_space=pl.ANY)],
        out_specs=pl.BlockSpec((tq, HEAD_DIM), lambda b, h, qi: (b * nq + qi, blk_out + h)),
        out_shape=jax.ShapeDtypeStruct(mix.shape, mix.dtype),
        scratch_shapes=[pltpu.VMEM((seq, 2 * HEAD_DIM), BF16), pltpu.VMEM((seq, 2 * HEAD_DIM), BF16),
                        pltpu.VMEM((tq, 2 * HEAD_DIM), BF16), pltpu.VMEM((tq, 1), F32),
                        pltpu.VMEM((tq, 2 * HEAD_DIM), F32)],
        input_output_aliases={5: 0},
        compiler_params=_params(("parallel", "parallel", "arbitrary")),
        name="fox_attention",
    )(proj, proj, proj, gates, norm_c.reshape(1, HEAD_DIM), mix)


def kernel(x, c, w_ada, b_ada, ln_g, ln_b, w_ffn_in, w_ffn_out, w_in, conv_w, a_log, dt_bias, norm_a,
           sgu_ln_g, sgu_ln_b, w_s, b_s, b_f, norm_c, w_o):
    bsz, seq, d = x.shape
    depth = w_ada.shape[0]
    m = bsz * seq
    h_a, h_c = a_log.shape[1], b_f.shape[1]
    d_a, d_c = h_a * HEAD_DIM, h_c * HEAD_DIM
    d_b = sgu_ln_g.shape[1]
    alpha = (2.0 * depth) ** 0.25
    assert 2 * h_a + h_c <= LANES and seq % DN_CHUNK == 0 and seq % w_s.shape[-1] == 0
    assert w_in.shape[-1] == 4 * d_a + 2 * h_a + 2 * d_b + 3 * d_c + h_c and d_a + d_b + d_c == d

    c_pad = jnp.zeros((SUBLANES, d), F32).at[:bsz].set(c)
    mod = _ada(c_pad, w_ada, b_ada)[:, :bsz]

    def mods(l):
        return [v.reshape(bsz, 1, d) for v in jnp.split(mod[l], 9, axis=-1)]

    w_big, w_small = _win_prep(w_in, d_a, d_b, d_c, h_a, h_c)
    w_ffn_in16 = w_ffn_in.astype(BF16)
    w_ffn_out16 = w_ffn_out.astype(BF16)
    w_o16 = w_o.astype(BF16)
    col_uv = 4 * d_a
    col_c = col_uv + 2 * d_b

    x2 = x.reshape(m, d)
    h = None
    for l in range(depth):
        sh1, sc1, ga1, sh2, sc2, ga2, sh3, sc3, ga3 = mods(l)
        if l == 0:
            h = _modulate(x2, sh1, sc1, seq)

        act = _swiglu_in(h, w_ffn_in16, (l, 0))
        y = _matmul_ktiled(act, w_ffn_out16, (l, 0), BF16, name="ffn_out")
        x2, h = _resid_ln(x2, y, ga1, ln_g[l, 0], ln_b[l, 0], (sh2, sc2), seq, alpha, 0.5)

        proj = _matmul(h, w_big, (l,), BF16, tn=512, name="mixer_in")
        small = _matmul(h, w_small, (l,), F32, tn=LANES, name="mixer_in_small")
        gates, gates_t = _gates(small, a_log[l], dt_bias[l], b_f[l], bsz, seq)
        mix = jnp.zeros((m, d), BF16)
        mix = _deltanet(proj, mix, conv_w[l], gates, gates_t, norm_a[l], bsz, seq, h_a)
        mix = _sgu(proj, mix, col_uv, d_a, d_b, sgu_ln_g[l], sgu_ln_b[l], w_s[l], b_s[l])
        mix = _fox(proj, mix, col_c, d_a + d_b, gates, norm_c[l], bsz, seq, h_c, 2 * h_a)
        y = _matmul(mix, w_o16, (l,), BF16, name="mixer_out")
        x2, h = _resid_ln(x2, y, ga2, ln_g[l, 1], ln_b[l, 1], (sh3, sc3), seq, alpha, 1.0)

        act = _swiglu_in(h, w_ffn_in16, (l, 1))
        y = _matmul_ktiled(act, w_ffn_out16, (l, 1), BF16, name="ffn_out")
        nxt = None
        if l + 1 < depth:
            nsh, nsc = mods(l + 1)[:2]
            nxt = (nsh, nsc)
        x2, h = _resid_ln(x2, y, ga3, ln_g[l, 2], ln_b[l, 2], nxt, seq, alpha, 0.5)
    return x2.reshape(bsz, seq, d)
```

```python
import functools
import math

import jax
import jax.numpy as jnp
from jax import lax
from jax.experimental import pallas as pl
from jax.experimental.pallas import tpu as pltpu

F32 = jnp.float32
BF16 = jnp.bfloat16

HEAD_DIM = 128
DN_CHUNK = 64
LN_EPS = 1e-5
RMS_EPS = 1e-6
LOG2E = 1.4426950408889634
LANES = 128
SUBLANES = 8
VMEM_LIMIT = 56 * 1024 * 1024


def _params(sem, vmem=VMEM_LIMIT):
    return pltpu.CompilerParams(dimension_semantics=sem, vmem_limit_bytes=vmem)


def _tile(n, pref):
    t = min(n, pref)
    while n % t:
        t //= 2
    return t


def _silu(x):
    return x * jax.nn.sigmoid(x)


def _softplus(x):
    return jnp.maximum(x, 0.0) + jnp.log1p(jnp.exp(-jnp.abs(x)))


def _dot(a, b):
    return jnp.dot(a, b, preferred_element_type=F32)


def _dot_nt(a, b):
    return lax.dot_general(a, b, (((1,), (1,)), ((), ())), preferred_element_type=F32)


def _dot_tn(a, b):
    return lax.dot_general(a, b, (((0,), (0,)), ((), ())), preferred_element_type=F32)


def _pick_lane(x, idx):
    lane = lax.broadcasted_iota(jnp.int32, x.shape, 1)
    return jnp.sum(jnp.where(lane == idx, x, 0.0), axis=-1, keepdims=True)


def _ada_kernel(c_ref, w_ref, b_ref, o_ref):
    c = c_ref[...]
    o_ref[...] = _dot(_silu(c).astype(BF16), w_ref[...].astype(BF16)) + b_ref[...]


def _ada(c_pad, w_ada, b_ada):
    depth, d, n = w_ada.shape
    bp = c_pad.shape[0]
    tn = _tile(n, 512)
    return pl.pallas_call(
        _ada_kernel,
        grid=(depth, n // tn),
        in_specs=[
            pl.BlockSpec((bp, d), lambda l, j: (0, 0)),
            pl.BlockSpec((None, d, tn), lambda l, j: (l, 0, j)),
            pl.BlockSpec((None, 1, tn), lambda l, j: (l, 0, j)),
        ],
        out_specs=pl.BlockSpec((None, bp, tn), lambda l, j: (l, 0, j)),
        out_shape=jax.ShapeDtypeStruct((depth, bp, n), F32),
        compiler_params=_params(("parallel", "parallel")),
        name="ada_mod",
    )(c_pad, w_ada, b_ada.reshape(depth, 1, n))


def _modulate_kernel(x_ref, sh_ref, sc_ref, h_ref):
    h_ref[...] = (x_ref[...] * (1.0 + sc_ref[...]) + sh_ref[...]).astype(h_ref.dtype)


def _modulate(x2, sh, sc, seq):
    m, d = x2.shape
    tm = _tile(seq, 512)
    per = seq // tm
    vec = pl.BlockSpec((None, 1, d), lambda i: (i // per, 0, 0))
    return pl.pallas_call(
        _modulate_kernel,
        grid=(m // tm,),
        in_specs=[pl.BlockSpec((tm, d), lambda i: (i, 0)), vec, vec],
        out_specs=pl.BlockSpec((tm, d), lambda i: (i, 0)),
        out_shape=jax.ShapeDtypeStruct((m, d), BF16),
        compiler_params=_params(("parallel",)),
        name="modulate",
    )(x2, sh, sc)


def _win_big_kernel(lo_ref, hi_ref, o_ref, *, n_aligned, group, keep):
    j = pl.program_id(1)

    @pl.when(j < n_aligned)
    def _():
        o_ref[...] = lo_ref[...].astype(o_ref.dtype)

    @pl.when(j >= n_aligned)
    def _():
        lane = lax.broadcasted_iota(jnp.int32, (lo_ref.shape[0], LANES), 1)
        for b in range(group):
            lo = lo_ref[:, b * LANES:(b + 1) * LANES]
            hi = lo_ref[:, (b + 1) * LANES:(b + 2) * LANES] if b + 1 < group else hi_ref[...]
            shifted = jnp.where(lane < keep, pltpu.roll(lo, keep, 1), pltpu.roll(hi, keep, 1))
            o_ref[:, b * LANES:(b + 1) * LANES] = shifted.astype(o_ref.dtype)


def _win_small_kernel(a_ref, b_ref, o_ref, *, n0, n1):
    lane = lax.broadcasted_iota(jnp.int32, a_ref.shape, 1)
    o_ref[...] = jnp.where(lane < n0, a_ref[...], jnp.where(lane < n0 + n1, b_ref[...], 0.0)).astype(o_ref.dtype)


def _win_prep(w_in, d_a, d_b, d_c, h_a, h_c):
    depth, d, _ = w_in.shape
    n_al = 4 * d_a // LANES
    n_sh = (2 * d_b + 3 * d_c) // LANES
    shift = 2 * h_a
    group = math.gcd(math.gcd(n_al, n_sh), 4)
    cw = group * LANES
    big = pl.pallas_call(
        functools.partial(_win_big_kernel, n_aligned=n_al // group, group=group, keep=LANES - shift),
        grid=(depth, (n_al + n_sh) // group),
        in_specs=[pl.BlockSpec((None, d, cw), lambda l, j: (l, 0, j)),
                  pl.BlockSpec((None, d, LANES), lambda l, j: (l, 0, (j + 1) * group))],
        out_specs=pl.BlockSpec((None, d, cw), lambda l, j: (l, 0, j)),
        out_shape=jax.ShapeDtypeStruct((depth, d, (n_al + n_sh) * LANES), BF16),
        compiler_params=_params(("parallel", "parallel")),
        name="w_in_relayout",
    )(w_in, w_in)
    small = pl.pallas_call(
        functools.partial(_win_small_kernel, n0=shift, n1=h_c),
        grid=(depth,),
        in_specs=[pl.BlockSpec((None, d, LANES), lambda l: (l, 0, n_al)),
                  pl.BlockSpec((None, d, LANES), lambda l: (l, 0, n_al + n_sh))],
        out_specs=pl.BlockSpec((None, d, LANES), lambda l: (l, 0, 0)),
        out_shape=jax.ShapeDtypeStruct((depth, d, LANES), BF16),
        compiler_params=_params(("parallel",)),
        name="w_in_small",
    )(w_in, w_in)
    return big, small


def _mm_kernel(a_ref, w_ref, o_ref):
    o_ref[...] = _dot(a_ref[...], w_ref[...]).astype(o_ref.dtype)


def _wspec(lead, rows, cols, index):
    return pl.BlockSpec((None,) * len(lead) + (rows, cols), lambda *g: tuple(lead) + tuple(index(*g)))


def _matmul(a, w, lead, out_dtype, tm=1024, tn=1024, name="matmul"):
    m, k = a.shape
    n = w.shape[-1]
    tm, tn = _tile(m, tm), _tile(n, tn)
    return pl.pallas_call(
        _mm_kernel,
        grid=(m // tm, n // tn),
        in_specs=[pl.BlockSpec((tm, k), lambda i, j: (i, 0)),
                  _wspec(lead, k, tn, lambda i, j: (0, j))],
        out_specs=pl.BlockSpec((tm, tn), lambda i, j: (i, j)),
        out_shape=jax.ShapeDtypeStruct((m, n), out_dtype),
        compiler_params=_params(("parallel", "parallel")),
        name=name,
    )(a, w)


def _swiglu_kernel(a_ref, wg_ref, wu_ref, o_ref):
    a = a_ref[...]
    g = _dot(a, wg_ref[...])
    u = _dot(a, wu_ref[...])
    o_ref[...] = (_silu(g) * u).astype(o_ref.dtype)


def _swiglu_in(h, w_in, lead, tm=1024, tf=512):
    m, k = h.shape
    f = w_in.shape[-1] // 2
    tm, tf = _tile(m, tm), _tile(f, tf)
    nf = f // tf
    return pl.pallas_call(
        _swiglu_kernel,
        grid=(m // tm, nf),
        in_specs=[pl.BlockSpec((tm, k), lambda i, j: (i, 0)),
                  _wspec(lead, k, tf, lambda i, j: (0, j)),
                  _wspec(lead, k, tf, lambda i, j: (0, j + nf))],
        out_specs=pl.BlockSpec((tm, tf), lambda i, j: (i, j)),
        out_shape=jax.ShapeDtypeStruct((m, f), BF16),
        compiler_params=_params(("parallel", "parallel")),
        name="ffn_in_swiglu",
    )(h, w_in, w_in)


def _mm_acc_kernel(a_ref, w_ref, o_ref, acc_ref):
    kk = pl.program_id(2)

    @pl.when(kk == 0)
    def _():
        acc_ref[...] = jnp.zeros_like(acc_ref)

    acc_ref[...] += _dot(a_ref[...], w_ref[...])

    @pl.when(kk == pl.num_programs(2) - 1)
    def _():
        o_ref[...] = acc_ref[...].astype(o_ref.dtype)


def _matmul_ktiled(a, w, lead, out_dtype, tm=1024, tn=1024, tk=4096, name="matmul_k"):
    m, k = a.shape
    n = w.shape[-1]
    tm, tn, tk = _tile(m, tm), _tile(n, tn), _tile(k, tk)
    return pl.pallas_call(
        _mm_acc_kernel,
        grid=(m // tm, n // tn, k // tk),
        in_specs=[pl.BlockSpec((tm, tk), lambda i, j, q: (i, q)),
                  _wspec(lead, tk, tn, lambda i, j, q: (q, j))],
        out_specs=pl.BlockSpec((tm, tn), lambda i, j, q: (i, j)),
        out_shape=jax.ShapeDtypeStruct((m, n), out_dtype),
        scratch_shapes=[pltpu.VMEM((tm, tn), F32)],
        compiler_params=_params(("parallel", "parallel", "arbitrary")),
        name=name,
    )(a, w)


def _resid_ln_kernel(x_ref, y_ref, ga_ref, g_ref, b_ref, *rest, alpha, coef, with_next):
    if with_next:
        sh_ref, sc_ref, xo_ref, h_ref = rest
    else:
        (xo_ref,) = rest
    r = alpha * x_ref[...] + (coef * ga_ref[...]) * y_ref[...].astype(F32)
    mu = jnp.mean(r, axis=-1, keepdims=True)
    rc = r - mu
    var = jnp.mean(rc * rc, axis=-1, keepdims=True)
    xn = rc * lax.rsqrt(var + LN_EPS) * g_ref[...] + b_ref[...]
    xo_ref[...] = xn
    if with_next:
        h_ref[...] = (xn * (1.0 + sc_ref[...]) + sh_ref[...]).astype(h_ref.dtype)


def _resid_ln(x2, y, ga, ln_g, ln_b, nxt, seq, alpha, coef):
    m, d = x2.shape
    tm = _tile(seq, 256)
    per = seq // tm
    row = pl.BlockSpec((tm, d), lambda i: (i, 0))
    vec = pl.BlockSpec((None, 1, d), lambda i: (i // per, 0, 0))
    par = pl.BlockSpec((1, d), lambda i: (0, 0))
    with_next = nxt is not None
    in_specs = [row, row, vec, par, par]
    args = [x2, y, ga, ln_g.reshape(1, d), ln_b.reshape(1, d)]
    out_specs = [row]
    out_shape = [jax.ShapeDtypeStruct((m, d), F32)]
    if with_next:
        in_specs += [vec, vec]
        args += list(nxt)
        out_specs.append(row)
        out_shape.append(jax.ShapeDtypeStruct((m, d), BF16))
    res = pl.pallas_call(
        functools.partial(_resid_ln_kernel, alpha=alpha, coef=coef, with_next=with_next),
        grid=(m // tm,),
        in_specs=in_specs,
        out_specs=out_specs,
        out_shape=out_shape,
        compiler_params=_params(("parallel",)),
        name="resid_ln",
    )(*args)
    return (res[0], res[1]) if with_next else (res[0], None)


def _gates_kernel(s_ref, p_ref, g_ref, gt_ref, carry_ref, *, h_a, h_c, tile):
    t = pl.program_id(1)

    @pl.when(t == 0)
    def _():
        carry_ref[...] = jnp.zeros_like(carry_ref)

    x = s_ref[...]
    a_log = p_ref[0:1, :]
    dt_b = p_ref[1:2, :]
    b_f = p_ref[2:3, :]
    lane = lax.broadcasted_iota(jnp.int32, x.shape, 1)
    beta = jax.nn.sigmoid(x)
    g = -jnp.exp(a_log) * _softplus(x + dt_b)
    lf = -_softplus(-(x + b_f))
    is_g = (lane >= h_a) & (lane < 2 * h_a)
    is_f = (lane >= 2 * h_a) & (lane < 2 * h_a + h_c)
    g = jnp.where(is_g, g, 0.0)
    lf = jnp.where(is_f, lf, 0.0)
    r = lax.broadcasted_iota(jnp.int32, (tile, tile), 0)
    c = lax.broadcasted_iota(jnp.int32, (tile, tile), 1)
    tri = r >= c
    tril = jnp.where(tri, 1.0, 0.0).astype(F32)
    btril = jnp.where(tri & ((r // DN_CHUNK) == (c // DN_CHUNK)), 1.0, 0.0).astype(F32)
    hi = lax.Precision.HIGHEST
    gcum = jnp.dot(btril, g, precision=hi, preferred_element_type=F32)
    fcum = jnp.dot(tril, lf, precision=hi, preferred_element_type=F32) + carry_ref[0:1, :]
    carry_ref[0:1, :] = fcum[tile - 1:tile, :]
    out = jnp.where(lane < h_a, beta, jnp.where(is_g, gcum, jnp.where(is_f, fcum, 0.0)))
    g_ref[...] = out
    gt_ref[...] = out.T


def _gates(small, a_log, dt_bias, b_f, bsz, seq):
    m = small.shape[0]
    h_a, h_c = a_log.shape[0], b_f.shape[0]
    tile = _tile(seq, 512)
    nt = seq // tile
    p = jnp.zeros((SUBLANES, LANES), F32)
    p = p.at[0, h_a:2 * h_a].set(a_log).at[1, h_a:2 * h_a].set(dt_bias).at[2, 2 * h_a:2 * h_a + h_c].set(b_f)
    return pl.pallas_call(
        functools.partial(_gates_kernel, h_a=h_a, h_c=h_c, tile=tile),
        grid=(bsz, nt),
        in_specs=[pl.BlockSpec((tile, LANES), lambda b, t: (b * nt + t, 0)),
                  pl.BlockSpec((SUBLANES, LANES), lambda b, t: (0, 0))],
        out_specs=[pl.BlockSpec((tile, LANES), lambda b, t: (b * nt + t, 0)),
                   pl.BlockSpec((None, LANES, tile), lambda b, t: (b, 0, t))],
        out_shape=[jax.ShapeDtypeStruct((m, LANES), F32),
                   jax.ShapeDtypeStruct((bsz, LANES, seq), F32)],
        scratch_shapes=[pltpu.VMEM((SUBLANES, LANES), F32)],
        compiler_params=_params(("arbitrary", "arbitrary")),
        name="gates",
    )(small, p)


def _split2(x):
    hi = x.astype(BF16)
    return hi, (x - hi.astype(F32)).astype(BF16)


def _delta_heads(q, k, v, beta, gc, gr, state, masks):
    same, tri, strict, eye = masks
    heads = range(len(q))
    tile = q[0].shape[0]
    decay = [jnp.where(tri, jnp.exp(jnp.where(tri, gc[h] - gr[h], 0.0)), 0.0) for h in heads]
    kb = [k[h] * beta[h] for h in heads]
    k16 = [k[h].astype(BF16) for h in heads]
    kk = [_dot_nt(kb[h].astype(BF16), k16[h]) * decay[h] for h in heads]
    lmat = [jnp.where(strict, kk[h], 0.0) for h in heads]
    xp = [-lmat[h] for h in heads]
    tinv = [eye + xp[h] for h in heads]
    for _ in range(int(math.log2(DN_CHUNK)) - 1):
        xp16 = [xp[h].astype(BF16) for h in heads]
        xp = [_dot(xp16[h], xp16[h]) for h in heads]
        tinv = [tinv[h] + _dot(tinv[h].astype(BF16), xp[h].astype(BF16)) for h in heads]
    tinv16 = [tinv[h].astype(BF16) for h in heads]
    egc = [jnp.exp(gc[h]) for h in heads]
    rhs = [jnp.concatenate([v[h] * beta[h], kb[h] * egc[h]], axis=1) for h in heads]
    uw = [_dot(tinv16[h], rhs[h].astype(BF16)) for h in heads]
    l_hi, l_lo = zip(*[_split2(lmat[h]) for h in heads])
    u_hi, u_lo = zip(*[_split2(uw[h]) for h in heads])
    lu = [_dot(l_hi[h], u_hi[h]) + _dot(l_hi[h], u_lo[h]) + _dot(l_lo[h], u_hi[h]) for h in heads]
    res = [rhs[h] - uw[h] - lu[h] for h in heads]
    uw = [uw[h] + _dot(tinv16[h], res[h].astype(BF16)) for h in heads]
    qk = [_dot_nt(q[h].astype(BF16), k16[h]) * decay[h] for h in heads]
    qe = [q[h] * egc[h] for h in heads]

    state = list(state)
    v_new = [[] for _ in heads]
    o_inter = [[] for _ in heads]
    for i in range(tile // DN_CHUNK):
        lo, hi = i * DN_CHUNK, (i + 1) * DN_CHUNK
        s16 = [state[h].astype(BF16) for h in heads]
        vn = [uw[h][lo:hi, :HEAD_DIM] - _dot(uw[h][lo:hi, HEAD_DIM:].astype(BF16), s16[h]) for h in heads]
        for h in heads:
            o_inter[h].append(_dot(qe[h][lo:hi].astype(BF16), s16[h]))
            v_new[h].append(vn[h])
        g_last = [gc[h][hi - 1:hi, :] for h in heads]
        kd = [k[h][lo:hi] * jnp.exp(g_last[h] - gc[h][lo:hi]) for h in heads]
        state = [state[h] * jnp.exp(g_last[h]) + _dot_tn(kd[h].astype(BF16), vn[h].astype(BF16)) for h in heads]
    o = [jnp.concatenate(o_inter[h], axis=0)
         + _dot(qk[h].astype(BF16), jnp.concatenate(v_new[h], axis=0).astype(BF16)) for h in heads]
    return o, state


def _deltanet_kernel(q_ref, k_ref, v_ref, z_ref, wq_ref, wk_ref, wv_ref, g_ref, gt_ref, na_ref, mix_ref,
                     o_ref, state_ref, tail_ref, *, h_a, hpg, tile, conv_k):
    del mix_ref
    hg = pl.program_id(1)
    t = pl.program_id(2)

    @pl.when(t == 0)
    def _():
        state_ref[...] = jnp.zeros_like(state_ref)
        tail_ref[...] = jnp.zeros_like(tail_ref)

    width = hpg * HEAD_DIM
    row8 = lax.broadcasted_iota(jnp.int32, (SUBLANES, width), 0)

    def conv_silu(x_ref, w_ref, slot):
        x = x_ref[...].astype(F32)
        w = w_ref[...]
        prev = tail_ref[slot]
        acc = x * w[conv_k - 1:conv_k, :]
        for sh in range(1, conv_k):
            xs = pltpu.roll(x, sh, 0)
            head = jnp.where(row8 < sh, pltpu.roll(prev, sh, 0), xs[:SUBLANES])
            xs = jnp.concatenate([head, xs[SUBLANES:]], axis=0)
            acc = acc + xs * w[conv_k - 1 - sh:conv_k - sh, :]
        tail_ref[slot] = x[tile - SUBLANES:, :]
        return _silu(acc)

    def l2n(x):
        return x * lax.rsqrt(jnp.sum(x * x, axis=-1, keepdims=True) + RMS_EPS)

    q_all = conv_silu(q_ref, wq_ref, 0)
    k_all = conv_silu(k_ref, wk_ref, 1)
    v_all = conv_silu(v_ref, wv_ref, 2)
    gates = g_ref[...]

    r = lax.broadcasted_iota(jnp.int32, (tile, tile), 0)
    c = lax.broadcasted_iota(jnp.int32, (tile, tile), 1)
    same = (r // DN_CHUNK) == (c // DN_CHUNK)
    masks = (same, same & (r >= c), same & (r > c), jnp.where(r == c, 1.0, 0.0).astype(F32))

    heads = range(hpg)
    cs = [slice(hh * HEAD_DIM, (hh + 1) * HEAD_DIM) for hh in heads]
    q = [l2n(q_all[:, cs[hh]]) * (HEAD_DIM ** -0.5) for hh in heads]
    k = [l2n(k_all[:, cs[hh]]) for hh in heads]
    v = [v_all[:, cs[hh]] for hh in heads]
    beta = [_pick_lane(gates, hg * hpg + hh) for hh in heads]
    gc = [_pick_lane(gates, h_a + hg * hpg + hh) for hh in heads]
    gr = [gt_ref[pl.ds(h_a + hg * hpg + hh, 1), :] for hh in heads]
    gate = _silu(z_ref[...].astype(F32))
    o, state = _delta_heads(q, k, v, beta, gc, gr, [state_ref[hh] for hh in heads], masks)
    for hh in heads:
        state_ref[hh] = state[hh]
        oh = o[hh] * lax.rsqrt(jnp.mean(o[hh] * o[hh], axis=-1, keepdims=True) + RMS_EPS) * na_ref[...]
        o_ref[:, cs[hh]] = (oh * gate[:, cs[hh]]).astype(o_ref.dtype)


def _deltanet(proj, mix, conv_w, gates, gates_t, norm_a, bsz, seq, h_a):
    conv_k = conv_w.shape[0]
    tile = _tile(seq, 256)
    nt = seq // tile
    hpg = next(g for g in (4, 3, 2, 1) if h_a % g == 0)
    ng = h_a // hpg
    width = hpg * HEAD_DIM

    def tok(off):
        return pl.BlockSpec((tile, width), lambda b, g, t: (b * nt + t, off * ng + g))

    def cw(off):
        return pl.BlockSpec((conv_k, width), lambda b, g, t: (0, off * ng + g))

    return pl.pallas_call(
        functools.partial(_deltanet_kernel, h_a=h_a, hpg=hpg, tile=tile, conv_k=conv_k),
        grid=(bsz, ng, nt),
        in_specs=[tok(0), tok(1), tok(2), tok(3), cw(0), cw(1), cw(2),
                  pl.BlockSpec((tile, LANES), lambda b, g, t: (b * nt + t, 0)),
                  pl.BlockSpec((None, LANES, tile), lambda b, g, t: (b, 0, t)),
                  pl.BlockSpec((1, HEAD_DIM), lambda b, g, t: (0, 0)),
                  pl.BlockSpec(memory_space=pl.ANY)],
        out_specs=pl.BlockSpec((tile, width), lambda b, g, t: (b * nt + t, g)),
        out_shape=jax.ShapeDtypeStruct(mix.shape, mix.dtype),
        scratch_shapes=[pltpu.VMEM((hpg, HEAD_DIM, HEAD_DIM), F32),
                        pltpu.VMEM((3, SUBLANES, width), F32)],
        input_output_aliases={10: 0},
        compiler_params=_params(("parallel", "parallel", "arbitrary")),
        name="deltanet",
    )(proj, proj, proj, proj, conv_w, conv_w, conv_w, gates, gates_t, norm_a.reshape(1, HEAD_DIM), mix)


def _gelu(x):
    return 0.5 * x * (1.0 + lax.erf(x * (2.0 ** -0.5)))


def _sgu_kernel(u_ref, vb_ref, v_ref, lg_ref, lb_ref, ws_ref, bs_ref, mix_ref, o_ref, *, chunk, tile, gpb):
    del mix_ref
    jb = pl.program_id(1)
    v = _gelu(v_ref[...].astype(F32))
    mu = jnp.mean(v, axis=-1, keepdims=True)
    vc = v - mu
    rstd = lax.rsqrt(jnp.mean(vc * vc, axis=-1, keepdims=True) + LN_EPS)
    u = _gelu(u_ref[...].astype(F32))
    vn = ((_gelu(vb_ref[...].astype(F32)) - mu) * rstd * lg_ref[...] + lb_ref[...]).astype(BF16)
    r = lax.broadcasted_iota(jnp.int32, (chunk, chunk), 0)
    c = lax.broadcasted_iota(jnp.int32, (chunk, chunk), 1)
    bs = bs_ref[...]
    for gl in range(gpb):
        g = jb * gpb + gl
        w = jnp.where(r >= c, ws_ref[g], 0.0).astype(BF16)
        bias = _pick_lane(bs, g)
        cs = slice(gl * HEAD_DIM, (gl + 1) * HEAD_DIM)
        for n in range(tile // chunk):
            rs = slice(n * chunk, (n + 1) * chunk)
            mixed = _dot(w, vn[rs, cs]) + bias
            o_ref[rs, cs] = (u[rs, cs] * mixed).astype(o_ref.dtype)


def _sgu(proj, mix, col_uv, col_out, d_b, ln_g, ln_b, w_s, b_s):
    m = proj.shape[0]
    groups, chunk, _ = w_s.shape
    tile = _tile(m, 256)
    ob = math.gcd(col_out, d_b)
    assert col_uv % d_b == 0 and ob % HEAD_DIM == 0
    nb = d_b // ob
    return pl.pallas_call(
        functools.partial(_sgu_kernel, chunk=chunk, tile=tile, gpb=ob // HEAD_DIM),
        grid=(m // tile, nb),
        in_specs=[pl.BlockSpec((tile, ob), lambda i, j: (i, col_uv // ob + j)),
                  pl.BlockSpec((tile, ob), lambda i, j: (i, (col_uv + d_b) // ob + j)),
                  pl.BlockSpec((tile, d_b), lambda i, j: (i, col_uv // d_b + 1)),
                  pl.BlockSpec((1, ob), lambda i, j: (0, j)),
                  pl.BlockSpec((1, ob), lambda i, j: (0, j)),
                  pl.BlockSpec((groups, chunk, chunk), lambda i, j: (0, 0, 0)),
                  pl.BlockSpec((chunk, groups), lambda i, j: (0, 0)),
                  pl.BlockSpec(memory_space=pl.ANY)],
        out_specs=pl.BlockSpec((tile, ob), lambda i, j: (i, col_out // ob + j)),
        out_shape=jax.ShapeDtypeStruct(mix.shape, mix.dtype),
        input_output_aliases={7: 0},
        compiler_params=_params(("parallel", "parallel")),
        name="sgu",
    )(proj, proj, proj, ln_g.reshape(1, d_b), ln_b.reshape(1, d_b), w_s, b_s.T, mix)


def _split3(x):
    hi = x.astype(BF16).astype(F32)
    r1 = x - hi
    mid = r1.astype(BF16).astype(F32)
    lo = (r1 - mid).astype(BF16).astype(F32)
    return hi, mid, lo


def _fox_kernel(q_ref, k_ref, v_ref, g_ref, nc_ref, mix_ref, o_ref, kaug_ref, vaug_ref, qaug_ref, m_ref, acc_ref,
                *, lane0, tq, rc, seq, prep):
    del mix_ref
    h = pl.program_id(1)
    qi = pl.program_id(2)

    @pl.when(qi == 0)
    def _():
        lane = lax.broadcasted_iota(jnp.int32, (prep, LANES), 1)
        ones0 = jnp.where(lane == 0, 1.0, 0.0).astype(BF16)
        for r0 in range(0, seq, prep):
            rows = slice(r0, r0 + prep)
            hi, mid, lo = _split3(_pick_lane(g_ref[rows, :], lane0 + h) * LOG2E)
            ex = jnp.where(lane < 3, 1.0,
                           jnp.where(lane == 3, -hi, jnp.where(lane == 4, -mid, jnp.where(lane == 5, -lo, 0.0))))
            kaug_ref[rows, :HEAD_DIM] = k_ref[rows, :]
            kaug_ref[rows, HEAD_DIM:] = ex.astype(BF16)
            vaug_ref[rows, :HEAD_DIM] = v_ref[rows, :]
            vaug_ref[rows, HEAD_DIM:] = ones0

    q0 = pl.multiple_of(qi * tq, tq)
    lane = lax.broadcasted_iota(jnp.int32, (tq, LANES), 1)
    hi, mid, lo = _split3(_pick_lane(g_ref[pl.ds(q0, tq), :], lane0 + h) * LOG2E)
    ex = jnp.where(lane == 0, hi, jnp.where(lane == 1, mid, jnp.where(lane == 2, lo, jnp.where(lane < 6, 1.0, 0.0))))
    qaug_ref[:, :HEAD_DIM] = (q_ref[...].astype(F32) * (HEAD_DIM ** -0.5 * LOG2E)).astype(BF16)
    qaug_ref[:, HEAD_DIM:] = ex.astype(BF16)
    m_ref[...] = jnp.full_like(m_ref, -jnp.inf)
    acc_ref[...] = jnp.zeros_like(acc_ref)

    nch = tq // rc
    row_id = lax.broadcasted_iota(jnp.int32, (rc, LANES), 0)
    col_id = lax.broadcasted_iota(jnp.int32, (rc, LANES), 1)

    def process(start, ncols, diag):
        rows = lambda r: slice(r * rc, (r + 1) * rc)
        scores = [_dot_nt(qaug_ref[rows(r), :], kaug_ref[pl.ds(start, ncols[r]), :]) for r in range(nch)]
        for r in range(nch):
            s, n = scores[r], ncols[r]
            blocks = [s[:, j * LANES:(j + 1) * LANES] for j in range(n // LANES)]
            if diag:
                first = (n - rc) // LANES
                for j in range(first, n // LANES):
                    lo = (j - first) * LANES
                    blocks[j] = jnp.where(col_id + lo <= row_id, blocks[j], -jnp.inf)
            m_prev = m_ref[rows(r), :]
            row_max = functools.reduce(jnp.maximum, blocks)
            m_new = jnp.maximum(m_prev, jnp.max(row_max, axis=-1, keepdims=True))
            p = jnp.concatenate([jnp.exp2(b - m_new) for b in blocks], axis=1).astype(BF16)
            alpha = jnp.exp2(m_prev - m_new)
            acc = acc_ref[rows(r), :]
            scaled = jnp.concatenate([acc[:, :HEAD_DIM] * alpha, acc[:, HEAD_DIM:] * alpha], axis=1)
            acc_ref[rows(r), :] = scaled + _dot(p, vaug_ref[pl.ds(start, n), :])
            m_ref[rows(r), :] = m_new

    def full_square(j, carry):
        process(pl.multiple_of(j * tq, tq), [tq] * nch, False)
        return carry

    lax.fori_loop(0, qi, full_square, 0)
    process(q0, [(r + 1) * rc for r in range(nch)], True)

    acc = acc_ref[...]
    o = acc[:, :HEAD_DIM] / acc[:, HEAD_DIM:HEAD_DIM + 1]
    o = o * lax.rsqrt(jnp.mean(o * o, axis=-1, keepdims=True) + RMS_EPS) * nc_ref[...]
    o_ref[...] = o.astype(o_ref.dtype)


def _fox(proj, mix, col_in, col_out, gates, norm_c, bsz, seq, h_c, lane0):
    tq = _tile(seq, 1024)
    rc = _tile(tq, 256)
    nq = seq // tq
    blk_in, blk_out = col_in // HEAD_DIM, col_out // HEAD_DIM
    prep = _tile(seq, 1024)
    return pl.pallas_call(
        functools.partial(_fox_kernel, lane0=lane0, tq=tq, rc=rc, seq=seq, prep=prep),
        grid=(bsz, h_c, nq),
        in_specs=[pl.BlockSpec((tq, HEAD_DIM), lambda b, h, qi: (b * nq + qi, blk_in + h)),
                  pl.BlockSpec((seq, HEAD_DIM), lambda b, h, qi: (b, blk_in + h_c + h)),
                  pl.BlockSpec((seq, HEAD_DIM), lambda b, h, qi: (b, blk_in + 2 * h_c + h)),
                  pl.BlockSpec((seq, LANES), lambda b, h, qi: (b, 0)),
                  pl.BlockSpec((1, HEAD_DIM), lambda b, h, qi: (0, 0)),
                  pl.BlockSpec(memory_space=pl.ANY)],
        out_specs=pl.BlockSpec((tq, HEAD_DIM), lambda b, h, qi: (b * nq + qi, blk_out + h)),
        out_shape=jax.ShapeDtypeStruct(mix.shape, mix.dtype),
        scratch_shapes=[pltpu.VMEM((seq, 2 * HEAD_DIM), BF16), pltpu.VMEM((seq, 2 * HEAD_DIM), BF16),
                        pltpu.VMEM((tq, 2 * HEAD_DIM), BF16), pltpu.VMEM((tq, LANES), F32),
                        pltpu.VMEM((tq, 2 * HEAD_DIM), F32)],
        input_output_aliases={5: 0},
        compiler_params=_params(("parallel", "parallel", "arbitrary")),
        name="fox_attention",
    )(proj, proj, proj, gates, norm_c.reshape(1, HEAD_DIM), mix)


def kernel(x, c, w_ada, b_ada, ln_g, ln_b, w_ffn_in, w_ffn_out, w_in, conv_w, a_log, dt_bias, norm_a,
           sgu_ln_g, sgu_ln_b, w_s, b_s, b_f, norm_c, w_o):
    bsz, seq, d = x.shape
    depth = w_ada.shape[0]
    m = bsz * seq
    h_a, h_c = a_log.shape[1], b_f.shape[1]
    d_a, d_c = h_a * HEAD_DIM, h_c * HEAD_DIM
    d_b = sgu_ln_g.shape[1]
    alpha = (2.0 * depth) ** 0.25
    assert 2 * h_a + h_c <= LANES and seq % DN_CHUNK == 0 and seq % w_s.shape[-1] == 0
    assert w_in.shape[-1] == 4 * d_a + 2 * h_a + 2 * d_b + 3 * d_c + h_c and d_a + d_b + d_c == d

    c_pad = jnp.zeros((SUBLANES, d), F32).at[:bsz].set(c)
    mod = _ada(c_pad, w_ada, b_ada)[:, :bsz]

    def mods(l):
        return [v.reshape(bsz, 1, d) for v in jnp.split(mod[l], 9, axis=-1)]

    w_big, w_small = _win_prep(w_in, d_a, d_b, d_c, h_a, h_c)
    w_ffn_in16 = w_ffn_in.astype(BF16)
    w_ffn_out16 = w_ffn_out.astype(BF16)
    w_o16 = w_o.astype(BF16)
    col_uv = 4 * d_a
    col_c = col_uv + 2 * d_b

    x2 = x.reshape(m, d)
    h = None
    for l in range(depth):
        sh1, sc1, ga1, sh2, sc2, ga2, sh3, sc3, ga3 = mods(l)
        if l == 0:
            h = _modulate(x2, sh1, sc1, seq)

        act = _swiglu_in(h, w_ffn_in16, (l, 0))
        y = _matmul_ktiled(act, w_ffn_out16, (l, 0), BF16, name="ffn_out")
        x2, h = _resid_ln(x2, y, ga1, ln_g[l, 0], ln_b[l, 0], (sh2, sc2), seq, alpha, 0.5)

        proj = _matmul(h, w_big, (l,), BF16, tn=512, name="mixer_in")
        small = _matmul(h, w_small, (l,), F32, tn=LANES, name="mixer_in_small")
        gates, gates_t = _gates(small, a_log[l], dt_bias[l], b_f[l], bsz, seq)
        mix = jnp.zeros((m, d), BF16)
        mix = _deltanet(proj, mix, conv_w[l], gates, gates_t, norm_a[l], bsz, seq, h_a)
        mix = _sgu(proj, mix, col_uv, d_a, d_b, sgu_ln_g[l], sgu_ln_b[l], w_s[l], b_s[l])
        mix = _fox(proj, mix, col_c, d_a + d_b, gates, norm_c[l], bsz, seq, h_c, 2 * h_a)
        y = _matmul(mix, w_o16, (l,), BF16, name="mixer_out")
        x2, h = _resid_ln(x2, y, ga2, ln_g[l, 1], ln_b[l, 1], (sh3, sc3), seq, alpha, 1.0)

        act = _swiglu_in(h, w_ffn_in16, (l, 1))
        y = _matmul_ktiled(act, w_ffn_out16, (l, 1), BF16, name="ffn_out")
        nxt = None
        if l + 1 < depth:
            nsh, nsc = mods(l + 1)[:2]
            nxt = (nsh, nsc)
        x2, h = _resid_ln(x2, y, ga3, ln_g[l, 2], ln_b[l, 2], nxt, seq, alpha, 0.5)
    return x2.reshape(bsz, seq, d)
```

```python
import functools
import math

import jax
import jax.numpy as jnp
from jax import lax
from jax.experimental import pallas as pl
from jax.experimental.pallas import tpu as pltpu

F32 = jnp.float32
BF16 = jnp.bfloat16

HEAD_DIM = 128
DN_CHUNK = 64
LN_EPS = 1e-5
RMS_EPS = 1e-6
LOG2E = 1.4426950408889634
LANES = 128
SUBLANES = 8
VMEM_LIMIT = 56 * 1024 * 1024


def _params(sem, vmem=VMEM_LIMIT):
    return pltpu.CompilerParams(dimension_semantics=sem, vmem_limit_bytes=vmem)


def _tile(n, pref):
    t = min(n, pref)
    while n % t:
        t //= 2
    return t


def _silu(x):
    return x * jax.nn.sigmoid(x)


def _softplus(x):
    return jnp.maximum(x, 0.0) + jnp.log1p(jnp.exp(-jnp.abs(x)))


def _dot(a, b):
    return jnp.dot(a, b, preferred_element_type=F32)


def _dot_nt(a, b):
    return lax.dot_general(a, b, (((1,), (1,)), ((), ())), preferred_element_type=F32)


def _dot_tn(a, b):
    return lax.dot_general(a, b, (((0,), (0,)), ((), ())), preferred_element_type=F32)


def _pick_lane(x, idx):
    lane = lax.broadcasted_iota(jnp.int32, x.shape, 1)
    return jnp.sum(jnp.where(lane == idx, x, 0.0), axis=-1, keepdims=True)


def _ada_kernel(c_ref, w_ref, b_ref, o_ref):
    c = c_ref[...]
    o_ref[...] = _dot(_silu(c).astype(BF16), w_ref[...].astype(BF16)) + b_ref[...]


def _ada(c_pad, w_ada, b_ada):
    depth, d, n = w_ada.shape
    bp = c_pad.shape[0]
    tn = _tile(n, 512)
    return pl.pallas_call(
        _ada_kernel,
        grid=(depth, n // tn),
        in_specs=[
            pl.BlockSpec((bp, d), lambda l, j: (0, 0)),
            pl.BlockSpec((None, d, tn), lambda l, j: (l, 0, j)),
            pl.BlockSpec((None, 1, tn), lambda l, j: (l, 0, j)),
        ],
        out_specs=pl.BlockSpec((None, bp, tn), lambda l, j: (l, 0, j)),
        out_shape=jax.ShapeDtypeStruct((depth, bp, n), F32),
        compiler_params=_params(("parallel", "parallel")),
        name="ada_mod",
    )(c_pad, w_ada, b_ada.reshape(depth, 1, n))


def _modulate_kernel(x_ref, sh_ref, sc_ref, h_ref):
    h_ref[...] = (x_ref[...] * (1.0 + sc_ref[...]) + sh_ref[...]).astype(h_ref.dtype)


def _modulate(x2, sh, sc, seq):
    m, d = x2.shape
    tm = _tile(seq, 512)
    per = seq // tm
    vec = pl.BlockSpec((None, 1, d), lambda i: (i // per, 0, 0))
    return pl.pallas_call(
        _modulate_kernel,
        grid=(m // tm,),
        in_specs=[pl.BlockSpec((tm, d), lambda i: (i, 0)), vec, vec],
        out_specs=pl.BlockSpec((tm, d), lambda i: (i, 0)),
        out_shape=jax.ShapeDtypeStruct((m, d), BF16),
        compiler_params=_params(("parallel",)),
        name="modulate",
    )(x2, sh, sc)


def _win_big_kernel(lo_ref, hi_ref, o_ref, *, n_aligned, group, keep):
    j = pl.program_id(1)

    @pl.when(j < n_aligned)
    def _():
        o_ref[...] = lo_ref[...].astype(o_ref.dtype)

    @pl.when(j >= n_aligned)
    def _():
        lane = lax.broadcasted_iota(jnp.int32, (lo_ref.shape[0], LANES), 1)
        for b in range(group):
            lo = lo_ref[:, b * LANES:(b + 1) * LANES]
            hi = lo_ref[:, (b + 1) * LANES:(b + 2) * LANES] if b + 1 < group else hi_ref[...]
            shifted = jnp.where(lane < keep, pltpu.roll(lo, keep, 1), pltpu.roll(hi, keep, 1))
            o_ref[:, b * LANES:(b + 1) * LANES] = shifted.astype(o_ref.dtype)


def _win_small_kernel(a_ref, b_ref, o_ref, *, n0, n1):
    lane = lax.broadcasted_iota(jnp.int32, a_ref.shape, 1)
    o_ref[...] = jnp.where(lane < n0, a_ref[...], jnp.where(lane < n0 + n1, b_ref[...], 0.0)).astype(o_ref.dtype)


def _win_prep(w_in, d_a, d_b, d_c, h_a, h_c):
    depth, d, _ = w_in.shape
    n_al = 4 * d_a // LANES
    n_sh = (2 * d_b + 3 * d_c) // LANES
    shift = 2 * h_a
    group = math.gcd(math.gcd(n_al, n_sh), 4)
    cw = group * LANES
    big = pl.pallas_call(
        functools.partial(_win_big_kernel, n_aligned=n_al // group, group=group, keep=LANES - shift),
        grid=(depth, (n_al + n_sh) // group),
        in_specs=[pl.BlockSpec((None, d, cw), lambda l, j: (l, 0, j)),
                  pl.BlockSpec((None, d, LANES), lambda l, j: (l, 0, (j + 1) * group))],
        out_specs=pl.BlockSpec((None, d, cw), lambda l, j: (l, 0, j)),
        out_shape=jax.ShapeDtypeStruct((depth, d, (n_al + n_sh) * LANES), BF16),
        compiler_params=_params(("parallel", "parallel")),
        name="w_in_relayout",
    )(w_in, w_in)
    small = pl.pallas_call(
        functools.partial(_win_small_kernel, n0=shift, n1=h_c),
        grid=(depth,),
        in_specs=[pl.BlockSpec((None, d, LANES), lambda l: (l, 0, n_al)),
                  pl.BlockSpec((None, d, LANES), lambda l: (l, 0, n_al + n_sh))],
        out_specs=pl.BlockSpec((None, d, LANES), lambda l: (l, 0, 0)),
        out_shape=jax.ShapeDtypeStruct((depth, d, LANES), BF16),
        compiler_params=_params(("parallel",)),
        name="w_in_small",
    )(w_in, w_in)
    return big, small


def _mm_kernel(a_ref, w_ref, o_ref):
    o_ref[...] = _dot(a_ref[...], w_ref[...]).astype(o_ref.dtype)


def _wspec(lead, rows, cols, index):
    return pl.BlockSpec((None,) * len(lead) + (rows, cols), lambda *g: tuple(lead) + tuple(index(*g)))


def _matmul(a, w, lead, out_dtype, tm=1024, tn=1024, name="matmul"):
    m, k = a.shape
    n = w.shape[-1]
    tm, tn = _tile(m, tm), _tile(n, tn)
    return pl.pallas_call(
        _mm_kernel,
        grid=(m // tm, n // tn),
        in_specs=[pl.BlockSpec((tm, k), lambda i, j: (i, 0)),
                  _wspec(lead, k, tn, lambda i, j: (0, j))],
        out_specs=pl.BlockSpec((tm, tn), lambda i, j: (i, j)),
        out_shape=jax.ShapeDtypeStruct((m, n), out_dtype),
        compiler_params=_params(("parallel", "parallel")),
        name=name,
    )(a, w)


def _swiglu_kernel(a_ref, wg_ref, wu_ref, o_ref):
    a = a_ref[...]
    g = _dot(a, wg_ref[...])
    u = _dot(a, wu_ref[...])
    o_ref[...] = (_silu(g) * u).astype(o_ref.dtype)


def _swiglu_in(h, w_in, lead, tm=1024, tf=512):
    m, k = h.shape
    f = w_in.shape[-1] // 2
    tm, tf = _tile(m, tm), _tile(f, tf)
    nf = f // tf
    return pl.pallas_call(
        _swiglu_kernel,
        grid=(m // tm, nf),
        in_specs=[pl.BlockSpec((tm, k), lambda i, j: (i, 0)),
                  _wspec(lead, k, tf, lambda i, j: (0, j)),
                  _wspec(lead, k, tf, lambda i, j: (0, j + nf))],
        out_specs=pl.BlockSpec((tm, tf), lambda i, j: (i, j)),
        out_shape=jax.ShapeDtypeStruct((m, f), BF16),
        compiler_params=_params(("parallel", "parallel")),
        name="ffn_in_swiglu",
    )(h, w_in, w_in)


def _mm_acc_kernel(a_ref, w_ref, o_ref, acc_ref):
    kk = pl.program_id(2)

    @pl.when(kk == 0)
    def _():
        acc_ref[...] = jnp.zeros_like(acc_ref)

    acc_ref[...] += _dot(a_ref[...], w_ref[...])

    @pl.when(kk == pl.num_programs(2) - 1)
    def _():
        o_ref[...] = acc_ref[...].astype(o_ref.dtype)


def _matmul_ktiled(a, w, lead, out_dtype, tm=1024, tn=1024, tk=4096, name="matmul_k"):
    m, k = a.shape
    n = w.shape[-1]
    tm, tn, tk = _tile(m, tm), _tile(n, tn), _tile(k, tk)
    return pl.pallas_call(
        _mm_acc_kernel,
        grid=(m // tm, n // tn, k // tk),
        in_specs=[pl.BlockSpec((tm, tk), lambda i, j, q: (i, q)),
                  _wspec(lead, tk, tn, lambda i, j, q: (q, j))],
        out_specs=pl.BlockSpec((tm, tn), lambda i, j, q: (i, j)),
        out_shape=jax.ShapeDtypeStruct((m, n), out_dtype),
        scratch_shapes=[pltpu.VMEM((tm, tn), F32)],
        compiler_params=_params(("parallel", "parallel", "arbitrary")),
        name=name,
    )(a, w)


def _resid_ln_kernel(x_ref, y_ref, ga_ref, g_ref, b_ref, *rest, alpha, coef, with_next):
    if with_next:
        sh_ref, sc_ref, xo_ref, h_ref = rest
    else:
        (xo_ref,) = rest
    r = alpha * x_ref[...] + (coef * ga_ref[...]) * y_ref[...].astype(F32)
    mu = jnp.mean(r, axis=-1, keepdims=True)
    rc = r - mu
    var = jnp.mean(rc * rc, axis=-1, keepdims=True)
    xn = rc * lax.rsqrt(var + LN_EPS) * g_ref[...] + b_ref[...]
    xo_ref[...] = xn
    if with_next:
        h_ref[...] = (xn * (1.0 + sc_ref[...]) + sh_ref[...]).astype(h_ref.dtype)


def _resid_ln(x2, y, ga, ln_g, ln_b, nxt, seq, alpha, coef):
    m, d = x2.shape
    tm = _tile(seq, 256)
    per = seq // tm
    row = pl.BlockSpec((tm, d), lambda i: (i, 0))
    vec = pl.BlockSpec((None, 1, d), lambda i: (i // per, 0, 0))
    par = pl.BlockSpec((1, d), lambda i: (0, 0))
    with_next = nxt is not None
    in_specs = [row, row, vec, par, par]
    args = [x2, y, ga, ln_g.reshape(1, d), ln_b.reshape(1, d)]
    out_specs = [row]
    out_shape = [jax.ShapeDtypeStruct((m, d), F32)]
    if with_next:
        in_specs += [vec, vec]
        args += list(nxt)
        out_specs.append(row)
        out_shape.append(jax.ShapeDtypeStruct((m, d), BF16))
    res = pl.pallas_call(
        functools.partial(_resid_ln_kernel, alpha=alpha, coef=coef, with_next=with_next),
        grid=(m // tm,),
        in_specs=in_specs,
        out_specs=out_specs,
        out_shape=out_shape,
        compiler_params=_params(("parallel",)),
        name="resid_ln",
    )(*args)
    return (res[0], res[1]) if with_next else (res[0], None)


def _gates_kernel(s_ref, p_ref, g_ref, gt_ref, carry_ref, *, h_a, h_c, tile):
    t = pl.program_id(1)

    @pl.when(t == 0)
    def _():
        carry_ref[...] = jnp.zeros_like(carry_ref)

    x = s_ref[...]
    a_log = p_ref[0:1, :]
    dt_b = p_ref[1:2, :]
    b_f = p_ref[2:3, :]
    lane = lax.broadcasted_iota(jnp.int32, x.shape, 1)
    beta = jax.nn.sigmoid(x)
    g = -jnp.exp(a_log) * _softplus(x + dt_b)
    lf = -_softplus(-(x + b_f))
    is_g = (lane >= h_a) & (lane < 2 * h_a)
    is_f = (lane >= 2 * h_a) & (lane < 2 * h_a + h_c)
    g = jnp.where(is_g, g, 0.0)
    lf = jnp.where(is_f, lf, 0.0)
    r = lax.broadcasted_iota(jnp.int32, (tile, tile), 0)
    c = lax.broadcasted_iota(jnp.int32, (tile, tile), 1)
    tri = r >= c
    tril = jnp.where(tri, 1.0, 0.0).astype(F32)
    btril = jnp.where(tri & ((r // DN_CHUNK) == (c // DN_CHUNK)), 1.0, 0.0).astype(F32)
    hi = lax.Precision.HIGHEST
    gcum = jnp.dot(btril, g, precision=hi, preferred_element_type=F32)
    fcum = jnp.dot(tril, lf, precision=hi, preferred_element_type=F32) + carry_ref[0:1, :]
    carry_ref[0:1, :] = fcum[tile - 1:tile, :]
    out = jnp.where(lane < h_a, beta, jnp.where(is_g, gcum, jnp.where(is_f, fcum, 0.0)))
    g_ref[...] = out
    gt_ref[...] = out.T


def _gates(small, a_log, dt_bias, b_f, bsz, seq):
    m = small.shape[0]
    h_a, h_c = a_log.shape[0], b_f.shape[0]
    tile = _tile(seq, 512)
    nt = seq // tile
    p = jnp.zeros((SUBLANES, LANES), F32)
    p = p.at[0, h_a:2 * h_a].set(a_log).at[1, h_a:2 * h_a].set(dt_bias).at[2, 2 * h_a:2 * h_a + h_c].set(b_f)
    return pl.pallas_call(
        functools.partial(_gates_kernel, h_a=h_a, h_c=h_c, tile=tile),
        grid=(bsz, nt),
        in_specs=[pl.BlockSpec((tile, LANES), lambda b, t: (b * nt + t, 0)),
                  pl.BlockSpec((SUBLANES, LANES), lambda b, t: (0, 0))],
        out_specs=[pl.BlockSpec((tile, LANES), lambda b, t: (b * nt + t, 0)),
                   pl.BlockSpec((None, LANES, tile), lambda b, t: (b, 0, t))],
        out_shape=[jax.ShapeDtypeStruct((m, LANES), F32),
                   jax.ShapeDtypeStruct((bsz, LANES, seq), F32)],
        scratch_shapes=[pltpu.VMEM((SUBLANES, LANES), F32)],
        compiler_params=_params(("arbitrary", "arbitrary")),
        name="gates",
    )(small, p)


def _split2(x):
    hi = x.astype(BF16)
    return hi, (x - hi.astype(F32)).astype(BF16)


def _delta_heads(q, k, v, beta, gc, gr, state, masks):
    same, tri, strict, eye = masks
    heads = range(len(q))
    tile = q[0].shape[0]
    decay = [jnp.where(tri, jnp.exp(jnp.where(tri, gc[h] - gr[h], 0.0)), 0.0) for h in heads]
    kb = [k[h] * beta[h] for h in heads]
    k16 = [k[h].astype(BF16) for h in heads]
    kk = [_dot_nt(kb[h].astype(BF16), k16[h]) * decay[h] for h in heads]
    lmat = [jnp.where(strict, kk[h], 0.0) for h in heads]
    xp = [-lmat[h] for h in heads]
    tinv = [eye + xp[h] for h in heads]
    for _ in range(int(math.log2(DN_CHUNK)) - 1):
        xp16 = [xp[h].astype(BF16) for h in heads]
        xp = [_dot(xp16[h], xp16[h]) for h in heads]
        tinv = [tinv[h] + _dot(tinv[h].astype(BF16), xp[h].astype(BF16)) for h in heads]
    tinv16 = [tinv[h].astype(BF16) for h in heads]
    egc = [jnp.exp(gc[h]) for h in heads]
    rhs = [jnp.concatenate([v[h] * beta[h], kb[h] * egc[h]], axis=1) for h in heads]
    uw = [_dot(tinv16[h], rhs[h].astype(BF16)) for h in heads]
    l_hi, l_lo = zip(*[_split2(lmat[h]) for h in heads])
    u_hi, u_lo = zip(*[_split2(uw[h]) for h in heads])
    lu = [_dot(l_hi[h], u_hi[h]) + _dot(l_hi[h], u_lo[h]) + _dot(l_lo[h], u_hi[h]) for h in heads]
    res = [rhs[h] - uw[h] - lu[h] for h in heads]
    uw = [uw[h] + _dot(tinv16[h], res[h].astype(BF16)) for h in heads]
    qk = [_dot_nt(q[h].astype(BF16), k16[h]) * decay[h] for h in heads]
    qe = [q[h] * egc[h] for h in heads]

    state = list(state)
    v_new = [[] for _ in heads]
    o_inter = [[] for _ in heads]
    for i in range(tile // DN_CHUNK):
        lo, hi = i * DN_CHUNK, (i + 1) * DN_CHUNK
        s16 = [state[h].astype(BF16) for h in heads]
        vn = [uw[h][lo:hi, :HEAD_DIM] - _dot(uw[h][lo:hi, HEAD_DIM:].astype(BF16), s16[h]) for h in heads]
        for h in heads:
            o_inter[h].append(_dot(qe[h][lo:hi].astype(BF16), s16[h]))
            v_new[h].append(vn[h])
        g_last = [gc[h][hi - 1:hi, :] for h in heads]
        kd = [k[h][lo:hi] * jnp.exp(g_last[h] - gc[h][lo:hi]) for h in heads]
        state = [state[h] * jnp.exp(g_last[h]) + _dot_tn(kd[h].astype(BF16), vn[h].astype(BF16)) for h in heads]
    o = [jnp.concatenate(o_inter[h], axis=0)
         + _dot(qk[h].astype(BF16), jnp.concatenate(v_new[h], axis=0).astype(BF16)) for h in heads]
    return o, state


def _deltanet_kernel(q_ref, k_ref, v_ref, z_ref, wq_ref, wk_ref, wv_ref, g_ref, gt_ref, na_ref, mix_ref,
                     o_ref, state_ref, tail_ref, *, h_a, hpg, tile, conv_k):
    del mix_ref
    hg = pl.program_id(1)
    t = pl.program_id(2)

    @pl.when(t == 0)
    def _():
        state_ref[...] = jnp.zeros_like(state_ref)
        tail_ref[...] = jnp.zeros_like(tail_ref)

    width = hpg * HEAD_DIM

    def conv_silu(x_ref, w_ref, slot):
        x = x_ref[...].astype(F32)
        w = w_ref[...]
        buf = tail_ref.at[slot]
        buf[SUBLANES:, :] = x
        acc = x * w[conv_k - 1:conv_k, :]
        for sh in range(1, conv_k):
            acc = acc + buf[SUBLANES - sh:SUBLANES - sh + tile, :] * w[conv_k - 1 - sh:conv_k - sh, :]
        buf[:SUBLANES, :] = x[tile - SUBLANES:, :]
        return _silu(acc)

    def l2n(x):
        return x * lax.rsqrt(jnp.sum(x * x, axis=-1, keepdims=True) + RMS_EPS)

    q_all = conv_silu(q_ref, wq_ref, 0)
    k_all = conv_silu(k_ref, wk_ref, 1)
    v_all = conv_silu(v_ref, wv_ref, 2)
    gates = g_ref[...]

    r = lax.broadcasted_iota(jnp.int32, (tile, tile), 0)
    c = lax.broadcasted_iota(jnp.int32, (tile, tile), 1)
    same = (r // DN_CHUNK) == (c // DN_CHUNK)
    masks = (same, same & (r >= c), same & (r > c), jnp.where(r == c, 1.0, 0.0).astype(F32))

    heads = range(hpg)
    cs = [slice(hh * HEAD_DIM, (hh + 1) * HEAD_DIM) for hh in heads]
    q = [l2n(q_all[:, cs[hh]]) * (HEAD_DIM ** -0.5) for hh in heads]
    k = [l2n(k_all[:, cs[hh]]) for hh in heads]
    v = [v_all[:, cs[hh]] for hh in heads]
    beta = [_pick_lane(gates, hg * hpg + hh) for hh in heads]
    gc = [_pick_lane(gates, h_a + hg * hpg + hh) for hh in heads]
    gr = [gt_ref[pl.ds(h_a + hg * hpg + hh, 1), :] for hh in heads]
    gate = _silu(z_ref[...].astype(F32))
    o, state = _delta_heads(q, k, v, beta, gc, gr, [state_ref[hh] for hh in heads], masks)
    for hh in heads:
        state_ref[hh] = state[hh]
        oh = o[hh] * lax.rsqrt(jnp.mean(o[hh] * o[hh], axis=-1, keepdims=True) + RMS_EPS) * na_ref[...]
        o_ref[:, cs[hh]] = (oh * gate[:, cs[hh]]).astype(o_ref.dtype)


def _deltanet(proj, mix, conv_w, gates, gates_t, norm_a, bsz, seq, h_a):
    conv_k = conv_w.shape[0]
    tile = _tile(seq, 256)
    nt = seq // tile
    hpg = next(g for g in (6, 4, 3, 2, 1) if h_a % g == 0)
    ng = h_a // hpg
    width = hpg * HEAD_DIM

    def tok(off):
        return pl.BlockSpec((tile, width), lambda b, g, t: (b * nt + t, off * ng + g))

    def cw(off):
        return pl.BlockSpec((conv_k, width), lambda b, g, t: (0, off * ng + g))

    return pl.pallas_call(
        functools.partial(_deltanet_kernel, h_a=h_a, hpg=hpg, tile=tile, conv_k=conv_k),
        grid=(bsz, ng, nt),
        in_specs=[tok(0), tok(1), tok(2), tok(3), cw(0), cw(1), cw(2),
                  pl.BlockSpec((tile, LANES), lambda b, g, t: (b * nt + t, 0)),
                  pl.BlockSpec((None, LANES, tile), lambda b, g, t: (b, 0, t)),
                  pl.BlockSpec((1, HEAD_DIM), lambda b, g, t: (0, 0)),
                  pl.BlockSpec(memory_space=pl.ANY)],
        out_specs=pl.BlockSpec((tile, width), lambda b, g, t: (b * nt + t, g)),
        out_shape=jax.ShapeDtypeStruct(mix.shape, mix.dtype),
        scratch_shapes=[pltpu.VMEM((hpg, HEAD_DIM, HEAD_DIM), F32),
                        pltpu.VMEM((3, SUBLANES + tile, width), F32)],
        input_output_aliases={10: 0},
        compiler_params=_params(("parallel", "parallel", "arbitrary")),
        name="deltanet",
    )(proj, proj, proj, proj, conv_w, conv_w, conv_w, gates, gates_t, norm_a.reshape(1, HEAD_DIM), mix)


def _gelu(x):
    return 0.5 * x * (1.0 + lax.erf(x * (2.0 ** -0.5)))


def _sgu_kernel(u_ref, vb_ref, v_ref, lg_ref, lb_ref, ws_ref, bs_ref, mix_ref, o_ref, *, chunk, tile, gpb):
    del mix_ref
    jb = pl.program_id(1)
    v = _gelu(v_ref[...].astype(F32))
    mu = jnp.mean(v, axis=-1, keepdims=True)
    vc = v - mu
    rstd = lax.rsqrt(jnp.mean(vc * vc, axis=-1, keepdims=True) + LN_EPS)
    u = _gelu(u_ref[...].astype(F32))
    vn = ((_gelu(vb_ref[...].astype(F32)) - mu) * rstd * lg_ref[...] + lb_ref[...]).astype(BF16)
    r = lax.broadcasted_iota(jnp.int32, (chunk, chunk), 0)
    c = lax.broadcasted_iota(jnp.int32, (chunk, chunk), 1)
    bs = bs_ref[...]
    for gl in range(gpb):
        g = jb * gpb + gl
        w = jnp.where(r >= c, ws_ref[g], 0.0).astype(BF16)
        bias = _pick_lane(bs, g)
        cs = slice(gl * HEAD_DIM, (gl + 1) * HEAD_DIM)
        for n in range(tile // chunk):
            rs = slice(n * chunk, (n + 1) * chunk)
            mixed = _dot(w, vn[rs, cs]) + bias
            o_ref[rs, cs] = (u[rs, cs] * mixed).astype(o_ref.dtype)


def _sgu(proj, mix, col_uv, col_out, d_b, ln_g, ln_b, w_s, b_s):
    m = proj.shape[0]
    groups, chunk, _ = w_s.shape
    tile = _tile(m, 256)
    ob = math.gcd(col_out, d_b)
    assert col_uv % d_b == 0 and ob % HEAD_DIM == 0
    nb = d_b // ob
    return pl.pallas_call(
        functools.partial(_sgu_kernel, chunk=chunk, tile=tile, gpb=ob // HEAD_DIM),
        grid=(m // tile, nb),
        in_specs=[pl.BlockSpec((tile, ob), lambda i, j: (i, col_uv // ob + j)),
                  pl.BlockSpec((tile, ob), lambda i, j: (i, (col_uv + d_b) // ob + j)),
                  pl.BlockSpec((tile, d_b), lambda i, j: (i, col_uv // d_b + 1)),
                  pl.BlockSpec((1, ob), lambda i, j: (0, j)),
                  pl.BlockSpec((1, ob), lambda i, j: (0, j)),
                  pl.BlockSpec((groups, chunk, chunk), lambda i, j: (0, 0, 0)),
                  pl.BlockSpec((chunk, groups), lambda i, j: (0, 0)),
                  pl.BlockSpec(memory_space=pl.ANY)],
        out_specs=pl.BlockSpec((tile, ob), lambda i, j: (i, col_out // ob + j)),
        out_shape=jax.ShapeDtypeStruct(mix.shape, mix.dtype),
        input_output_aliases={7: 0},
        compiler_params=_params(("parallel", "parallel")),
        name="sgu",
    )(proj, proj, proj, ln_g.reshape(1, d_b), ln_b.reshape(1, d_b), w_s, b_s.T, mix)


def _split3(x):
    hi = x.astype(BF16).astype(F32)
    r1 = x - hi
    mid = r1.astype(BF16).astype(F32)
    lo = (r1 - mid).astype(BF16).astype(F32)
    return hi, mid, lo


def _fox_kernel(q_ref, k_ref, v_ref, g_ref, nc_ref, mix_ref, o_ref, kaug_ref, vaug_ref, qaug_ref, m_ref, acc_ref,
                *, lane0, tq, rc, seq, prep):
    del mix_ref
    h = pl.program_id(1)
    qi = pl.program_id(2)

    @pl.when(qi == 0)
    def _():
        lane = lax.broadcasted_iota(jnp.int32, (prep, LANES), 1)
        ones0 = jnp.where(lane == 0, 1.0, 0.0).astype(BF16)
        for r0 in range(0, seq, prep):
            rows = slice(r0, r0 + prep)
            hi, mid, lo = _split3(_pick_lane(g_ref[rows, :], lane0 + h) * LOG2E)
            ex = jnp.where(lane < 3, 1.0,
                           jnp.where(lane == 3, -hi, jnp.where(lane == 4, -mid, jnp.where(lane == 5, -lo, 0.0))))
            kaug_ref[rows, :HEAD_DIM] = k_ref[rows, :]
            kaug_ref[rows, HEAD_DIM:] = ex.astype(BF16)
            vaug_ref[rows, :HEAD_DIM] = v_ref[rows, :]
            vaug_ref[rows, HEAD_DIM:] = ones0

    q0 = pl.multiple_of(qi * tq, tq)
    lane = lax.broadcasted_iota(jnp.int32, (tq, LANES), 1)
    hi, mid, lo = _split3(_pick_lane(g_ref[pl.ds(q0, tq), :], lane0 + h) * LOG2E)
    ex = jnp.where(lane == 0, hi, jnp.where(lane == 1, mid, jnp.where(lane == 2, lo, jnp.where(lane < 6, 1.0, 0.0))))
    qaug_ref[:, :HEAD_DIM] = (q_ref[...].astype(F32) * (HEAD_DIM ** -0.5 * LOG2E)).astype(BF16)
    qaug_ref[:, HEAD_DIM:] = ex.astype(BF16)
    m_ref[...] = jnp.full_like(m_ref, -jnp.inf)
    acc_ref[...] = jnp.zeros_like(acc_ref)

    nch = tq // rc
    row_id = lax.broadcasted_iota(jnp.int32, (rc, LANES), 0)
    col_id = lax.broadcasted_iota(jnp.int32, (rc, LANES), 1)

    def process(start, ncols, diag):
        rows = lambda r: slice(r * rc, (r + 1) * rc)
        scores = [_dot_nt(qaug_ref[rows(r), :], kaug_ref[pl.ds(start, ncols[r]), :]) for r in range(nch)]
        for r in range(nch):
            s, n = scores[r], ncols[r]
            blocks = [s[:, j * LANES:(j + 1) * LANES] for j in range(n // LANES)]
            if diag:
                first = (n - rc) // LANES
                for j in range(first, n // LANES):
                    lo = (j - first) * LANES
                    blocks[j] = jnp.where(col_id + lo <= row_id, blocks[j], -jnp.inf)
            m_prev = m_ref[rows(r), :]
            row_max = functools.reduce(jnp.maximum, blocks)
            m_new = jnp.maximum(m_prev, jnp.max(row_max, axis=-1, keepdims=True))
            p = jnp.concatenate([jnp.exp2(b - m_new) for b in blocks], axis=1).astype(BF16)
            alpha = jnp.exp2(m_prev - m_new)
            acc = acc_ref[rows(r), :]
            scaled = jnp.concatenate([acc[:, :HEAD_DIM] * alpha, acc[:, HEAD_DIM:] * alpha], axis=1)
            acc_ref[rows(r), :] = scaled + _dot(p, vaug_ref[pl.ds(start, n), :])
            m_ref[rows(r), :] = m_new

    def full_square(j, carry):
        process(pl.multiple_of(j * tq, tq), [tq] * nch, False)
        return carry

    lax.fori_loop(0, qi, full_square, 0)
    process(q0, [(r + 1) * rc for r in range(nch)], True)

    acc = acc_ref[...]
    o = acc[:, :HEAD_DIM] / acc[:, HEAD_DIM:HEAD_DIM + 1]
    o = o * lax.rsqrt(jnp.mean(o * o, axis=-1, keepdims=True) + RMS_EPS) * nc_ref[...]
    o_ref[...] = o.astype(o_ref.dtype)


def _fox(proj, mix, col_in, col_out, gates, norm_c, bsz, seq, h_c, lane0):
    tq = _tile(seq, 1024)
    rc = _tile(tq, 256)
    nq = seq // tq
    blk_in, blk_out = col_in // HEAD_DIM, col_out // HEAD_DIM
    prep = _tile(seq, 1024)
    return pl.pallas_call(
        functools.partial(_fox_kernel, lane0=lane0, tq=tq, rc=rc, seq=seq, prep=prep),
        grid=(bsz, h_c, nq),
        in_specs=[pl.BlockSpec((tq, HEAD_DIM), lambda b, h, qi: (b * nq + qi, blk_in + h)),
                  pl.BlockSpec((seq, HEAD_DIM), lambda b, h, qi: (b, blk_in + h_c + h)),
                  pl.BlockSpec((seq, HEAD_DIM), lambda b, h, qi: (b, blk_in + 2 * h_c + h)),
                  pl.BlockSpec((seq, LANES), lambda b, h, qi: (b, 0)),
                  pl.BlockSpec((1, HEAD_DIM), lambda b, h, qi: (0, 0)),
                  pl.BlockSpec(memory_space=pl.ANY)],
        out_specs=pl.BlockSpec((tq, HEAD_DIM), lambda b, h, qi: (b * nq + qi, blk_out + h)),
        out_shape=jax.ShapeDtypeStruct(mix.shape, mix.dtype),
        scratch_shapes=[pltpu.VMEM((seq, 2 * HEAD_DIM), BF16), pltpu.VMEM((seq, 2 * HEAD_DIM), BF16),
                        pltpu.VMEM((tq, 2 * HEAD_DIM), BF16), pltpu.VMEM((tq, LANES), F32),
                        pltpu.VMEM((tq, 2 * HEAD_DIM), F32)],
        input_output_aliases={5: 0},
        compiler_params=_params(("parallel", "parallel", "arbitrary")),
        name="fox_attention",
    )(proj, proj, proj, gates, norm_c.reshape(1, HEAD_DIM), mix)


def kernel(x, c, w_ada, b_ada, ln_g, ln_b, w_ffn_in, w_ffn_out, w_in, conv_w, a_log, dt_bias, norm_a,
           sgu_ln_g, sgu_ln_b, w_s, b_s, b_f, norm_c, w_o):
    bsz, seq, d = x.shape
    depth = w_ada.shape[0]
    m = bsz * seq
    h_a, h_c = a_log.shape[1], b_f.shape[1]
    d_a, d_c = h_a * HEAD_DIM, h_c * HEAD_DIM
    d_b = sgu_ln_g.shape[1]
    alpha = (2.0 * depth) ** 0.25
    assert 2 * h_a + h_c <= LANES and seq % DN_CHUNK == 0 and seq % w_s.shape[-1] == 0
    assert w_in.shape[-1] == 4 * d_a + 2 * h_a + 2 * d_b + 3 * d_c + h_c and d_a + d_b + d_c == d

    c_pad = jnp.zeros((SUBLANES, d), F32).at[:bsz].set(c)
    mod = _ada(c_pad, w_ada, b_ada)[:, :bsz]

    def mods(l):
        return [v.reshape(bsz, 1, d) for v in jnp.split(mod[l], 9, axis=-1)]

    w_big, w_small = _win_prep(w_in, d_a, d_b, d_c, h_a, h_c)
    w_ffn_in16 = w_ffn_in.astype(BF16)
    w_ffn_out16 = w_ffn_out.astype(BF16)
    w_o16 = w_o.astype(BF16)
    col_uv = 4 * d_a
    col_c = col_uv + 2 * d_b

    x2 = x.reshape(m, d)
    h = None
    for l in range(depth):
        sh1, sc1, ga1, sh2, sc2, ga2, sh3, sc3, ga3 = mods(l)
        if l == 0:
            h = _modulate(x2, sh1, sc1, seq)

        act = _swiglu_in(h, w_ffn_in16, (l, 0))
        y = _matmul_ktiled(act, w_ffn_out16, (l, 0), BF16, name="ffn_out")
        x2, h = _resid_ln(x2, y, ga1, ln_g[l, 0], ln_b[l, 0], (sh2, sc2), seq, alpha, 0.5)

        proj = _matmul(h, w_big, (l,), BF16, tn=1280, name="mixer_in")
        small = _matmul(h, w_small, (l,), F32, tn=LANES, name="mixer_in_small")
        gates, gates_t = _gates(small, a_log[l], dt_bias[l], b_f[l], bsz, seq)
        mix = jnp.zeros((m, d), BF16)
        mix = _deltanet(proj, mix, conv_w[l], gates, gates_t, norm_a[l], bsz, seq, h_a)
        mix = _sgu(proj, mix, col_uv, d_a, d_b, sgu_ln_g[l], sgu_ln_b[l], w_s[l], b_s[l])
        mix = _fox(proj, mix, col_c, d_a + d_b, gates, norm_c[l], bsz, seq, h_c, 2 * h_a)
        y = _matmul(mix, w_o16, (l,), BF16, name="mixer_out")
        x2, h = _resid_ln(x2, y, ga2, ln_g[l, 1], ln_b[l, 1], (sh3, sc3), seq, alpha, 1.0)

        act = _swiglu_in(h, w_ffn_in16, (l, 1))
        y = _matmul_ktiled(act, w_ffn_out16, (l, 1), BF16, name="ffn_out")
        nxt = None
        if l + 1 < depth:
            nsh, nsc = mods(l + 1)[:2]
            nxt = (nsh, nsc)
        x2, h = _resid_ln(x2, y, ga3, ln_g[l, 2], ln_b[l, 2], nxt, seq, alpha, 0.5)
    return x2.reshape(bsz, seq, d)
```

```python
import functools
import math

import jax
import jax.numpy as jnp
from jax import lax
from jax.experimental import pallas as pl
from jax.experimental.pallas import tpu as pltpu

F32 = jnp.float32
BF16 = jnp.bfloat16

HEAD_DIM = 128
DN_CHUNK = 64
LN_EPS = 1e-5
RMS_EPS = 1e-6
LOG2E = 1.4426950408889634
LANES = 128
SUBLANES = 8
VMEM_LIMIT = 56 * 1024 * 1024


def _params(sem, vmem=VMEM_LIMIT):
    return pltpu.CompilerParams(dimension_semantics=sem, vmem_limit_bytes=vmem)


def _tile(n, pref):
    t = min(n, pref)
    while n % t:
        t //= 2
    return t


def _silu(x):
    return x * jax.nn.sigmoid(x)


def _softplus(x):
    return jnp.maximum(x, 0.0) + jnp.log1p(jnp.exp(-jnp.abs(x)))


def _dot(a, b):
    return jnp.dot(a, b, preferred_element_type=F32)


def _dot_nt(a, b):
    return lax.dot_general(a, b, (((1,), (1,)), ((), ())), preferred_element_type=F32)


def _dot_tn(a, b):
    return lax.dot_general(a, b, (((0,), (0,)), ((), ())), preferred_element_type=F32)


def _pick_lane(x, idx):
    lane = lax.broadcasted_iota(jnp.int32, x.shape, 1)
    return jnp.sum(jnp.where(lane == idx, x, 0.0), axis=-1, keepdims=True)


def _ada_kernel(c_ref, w_ref, b_ref, o_ref):
    c = c_ref[...]
    o_ref[...] = _dot(_silu(c).astype(BF16), w_ref[...].astype(BF16)) + b_ref[...]


def _ada(c_pad, w_ada, b_ada):
    depth, d, n = w_ada.shape
    bp = c_pad.shape[0]
    tn = _tile(n, 512)
    return pl.pallas_call(
        _ada_kernel,
        grid=(depth, n // tn),
        in_specs=[
            pl.BlockSpec((bp, d), lambda l, j: (0, 0)),
            pl.BlockSpec((None, d, tn), lambda l, j: (l, 0, j)),
            pl.BlockSpec((None, 1, tn), lambda l, j: (l, 0, j)),
        ],
        out_specs=pl.BlockSpec((None, bp, tn), lambda l, j: (l, 0, j)),
        out_shape=jax.ShapeDtypeStruct((depth, bp, n), F32),
        compiler_params=_params(("parallel", "parallel")),
        name="ada_mod",
    )(c_pad, w_ada, b_ada.reshape(depth, 1, n))


def _modulate_kernel(x_ref, sh_ref, sc_ref, h_ref):
    h_ref[...] = (x_ref[...] * (1.0 + sc_ref[...]) + sh_ref[...]).astype(h_ref.dtype)


def _modulate(x2, sh, sc, seq):
    m, d = x2.shape
    tm = _tile(seq, 512)
    per = seq // tm
    vec = pl.BlockSpec((None, 1, d), lambda i: (i // per, 0, 0))
    return pl.pallas_call(
        _modulate_kernel,
        grid=(m // tm,),
        in_specs=[pl.BlockSpec((tm, d), lambda i: (i, 0)), vec, vec],
        out_specs=pl.BlockSpec((tm, d), lambda i: (i, 0)),
        out_shape=jax.ShapeDtypeStruct((m, d), BF16),
        compiler_params=_params(("parallel",)),
        name="modulate",
    )(x2, sh, sc)


def _win_big_kernel(lo_ref, hi_ref, o_ref, *, n_aligned, group, keep):
    j = pl.program_id(1)

    @pl.when(j < n_aligned)
    def _():
        o_ref[...] = lo_ref[...].astype(o_ref.dtype)

    @pl.when(j >= n_aligned)
    def _():
        lane = lax.broadcasted_iota(jnp.int32, (lo_ref.shape[0], LANES), 1)
        for b in range(group):
            lo = lo_ref[:, b * LANES:(b + 1) * LANES]
            hi = lo_ref[:, (b + 1) * LANES:(b + 2) * LANES] if b + 1 < group else hi_ref[...]
            shifted = jnp.where(lane < keep, pltpu.roll(lo, keep, 1), pltpu.roll(hi, keep, 1))
            o_ref[:, b * LANES:(b + 1) * LANES] = shifted.astype(o_ref.dtype)


def _win_small_kernel(a_ref, b_ref, o_ref, *, n0, n1):
    lane = lax.broadcasted_iota(jnp.int32, a_ref.shape, 1)
    o_ref[...] = jnp.where(lane < n0, a_ref[...], jnp.where(lane < n0 + n1, b_ref[...], 0.0)).astype(o_ref.dtype)


def _win_prep(w_in, d_a, d_b, d_c, h_a, h_c):
    depth, d, _ = w_in.shape
    n_al = 4 * d_a // LANES
    n_sh = (2 * d_b + 3 * d_c) // LANES
    shift = 2 * h_a
    group = math.gcd(math.gcd(n_al, n_sh), 4)
    cw = group * LANES
    big = pl.pallas_call(
        functools.partial(_win_big_kernel, n_aligned=n_al // group, group=group, keep=LANES - shift),
        grid=(depth, (n_al + n_sh) // group),
        in_specs=[pl.BlockSpec((None, d, cw), lambda l, j: (l, 0, j)),
                  pl.BlockSpec((None, d, LANES), lambda l, j: (l, 0, (j + 1) * group))],
        out_specs=pl.BlockSpec((None, d, cw), lambda l, j: (l, 0, j)),
        out_shape=jax.ShapeDtypeStruct((depth, d, (n_al + n_sh) * LANES), BF16),
        compiler_params=_params(("parallel", "parallel")),
        name="w_in_relayout",
    )(w_in, w_in)
    small = pl.pallas_call(
        functools.partial(_win_small_kernel, n0=shift, n1=h_c),
        grid=(depth,),
        in_specs=[pl.BlockSpec((None, d, LANES), lambda l: (l, 0, n_al)),
                  pl.BlockSpec((None, d, LANES), lambda l: (l, 0, n_al + n_sh))],
        out_specs=pl.BlockSpec((None, d, LANES), lambda l: (l, 0, 0)),
        out_shape=jax.ShapeDtypeStruct((depth, d, LANES), BF16),
        compiler_params=_params(("parallel",)),
        name="w_in_small",
    )(w_in, w_in)
    return big, small


def _mm_kernel(a_ref, w_ref, o_ref):
    o_ref[...] = _dot(a_ref[...], w_ref[...]).astype(o_ref.dtype)


def _wspec(lead, rows, cols, index):
    return pl.BlockSpec((None,) * len(lead) + (rows, cols), lambda *g: tuple(lead) + tuple(index(*g)))


def _matmul(a, w, lead, out_dtype, tm=1024, tn=1024, name="matmul"):
    m, k = a.shape
    n = w.shape[-1]
    tm, tn = _tile(m, tm), _tile(n, tn)
    return pl.pallas_call(
        _mm_kernel,
        grid=(m // tm, n // tn),
        in_specs=[pl.BlockSpec((tm, k), lambda i, j: (i, 0)),
                  _wspec(lead, k, tn, lambda i, j: (0, j))],
        out_specs=pl.BlockSpec((tm, tn), lambda i, j: (i, j)),
        out_shape=jax.ShapeDtypeStruct((m, n), out_dtype),
        compiler_params=_params(("parallel", "parallel")),
        name=name,
    )(a, w)


def _swiglu_kernel(a_ref, wg_ref, wu_ref, o_ref):
    a = a_ref[...]
    g = _dot(a, wg_ref[...])
    u = _dot(a, wu_ref[...])
    o_ref[...] = (_silu(g) * u).astype(o_ref.dtype)


def _swiglu_in(h, w_in, lead, tm=1024, tf=512):
    m, k = h.shape
    f = w_in.shape[-1] // 2
    tm, tf = _tile(m, tm), _tile(f, tf)
    nf = f // tf
    return pl.pallas_call(
        _swiglu_kernel,
        grid=(m // tm, nf),
        in_specs=[pl.BlockSpec((tm, k), lambda i, j: (i, 0)),
                  _wspec(lead, k, tf, lambda i, j: (0, j)),
                  _wspec(lead, k, tf, lambda i, j: (0, j + nf))],
        out_specs=pl.BlockSpec((tm, tf), lambda i, j: (i, j)),
        out_shape=jax.ShapeDtypeStruct((m, f), BF16),
        compiler_params=_params(("parallel", "parallel")),
        name="ffn_in_swiglu",
    )(h, w_in, w_in)


def _mm_acc_kernel(a_ref, w_ref, o_ref, acc_ref):
    kk = pl.program_id(2)

    @pl.when(kk == 0)
    def _():
        acc_ref[...] = jnp.zeros_like(acc_ref)

    acc_ref[...] += _dot(a_ref[...], w_ref[...])

    @pl.when(kk == pl.num_programs(2) - 1)
    def _():
        o_ref[...] = acc_ref[...].astype(o_ref.dtype)


def _matmul_ktiled(a, w, lead, out_dtype, tm=1024, tn=1024, tk=4096, name="matmul_k"):
    m, k = a.shape
    n = w.shape[-1]
    tm, tn, tk = _tile(m, tm), _tile(n, tn), _tile(k, tk)
    return pl.pallas_call(
        _mm_acc_kernel,
        grid=(m // tm, n // tn, k // tk),
        in_specs=[pl.BlockSpec((tm, tk), lambda i, j, q: (i, q)),
                  _wspec(lead, tk, tn, lambda i, j, q: (q, j))],
        out_specs=pl.BlockSpec((tm, tn), lambda i, j, q: (i, j)),
        out_shape=jax.ShapeDtypeStruct((m, n), out_dtype),
        scratch_shapes=[pltpu.VMEM((tm, tn), F32)],
        compiler_params=_params(("parallel", "parallel", "arbitrary")),
        name=name,
    )(a, w)


def _resid_ln_kernel(x_ref, y_ref, ga_ref, g_ref, b_ref, *rest, alpha, coef, with_next):
    if with_next:
        sh_ref, sc_ref, xo_ref, h_ref = rest
    else:
        (xo_ref,) = rest
    r = alpha * x_ref[...] + (coef * ga_ref[...]) * y_ref[...].astype(F32)
    mu = jnp.mean(r, axis=-1, keepdims=True)
    rc = r - mu
    var = jnp.mean(rc * rc, axis=-1, keepdims=True)
    xn = rc * lax.rsqrt(var + LN_EPS) * g_ref[...] + b_ref[...]
    xo_ref[...] = xn
    if with_next:
        h_ref[...] = (xn * (1.0 + sc_ref[...]) + sh_ref[...]).astype(h_ref.dtype)


def _resid_ln(x2, y, ga, ln_g, ln_b, nxt, seq, alpha, coef):
    m, d = x2.shape
    tm = _tile(seq, 256)
    per = seq // tm
    row = pl.BlockSpec((tm, d), lambda i: (i, 0))
    vec = pl.BlockSpec((None, 1, d), lambda i: (i // per, 0, 0))
    par = pl.BlockSpec((1, d), lambda i: (0, 0))
    with_next = nxt is not None
    in_specs = [row, row, vec, par, par]
    args = [x2, y, ga, ln_g.reshape(1, d), ln_b.reshape(1, d)]
    out_specs = [row]
    out_shape = [jax.ShapeDtypeStruct((m, d), F32)]
    if with_next:
        in_specs += [vec, vec]
        args += list(nxt)
        out_specs.append(row)
        out_shape.append(jax.ShapeDtypeStruct((m, d), BF16))
    res = pl.pallas_call(
        functools.partial(_resid_ln_kernel, alpha=alpha, coef=coef, with_next=with_next),
        grid=(m // tm,),
        in_specs=in_specs,
        out_specs=out_specs,
        out_shape=out_shape,
        compiler_params=_params(("parallel",)),
        name="resid_ln",
    )(*args)
    return (res[0], res[1]) if with_next else (res[0], None)


def _gates_kernel(s_ref, p_ref, g_ref, gt_ref, carry_ref, *, h_a, h_c, tile):
    t = pl.program_id(1)

    @pl.when(t == 0)
    def _():
        carry_ref[...] = jnp.zeros_like(carry_ref)

    x = s_ref[...]
    a_log = p_ref[0:1, :]
    dt_b = p_ref[1:2, :]
    b_f = p_ref[2:3, :]
    lane = lax.broadcasted_iota(jnp.int32, x.shape, 1)
    beta = jax.nn.sigmoid(x)
    g = -jnp.exp(a_log) * _softplus(x + dt_b)
    lf = -_softplus(-(x + b_f))
    is_g = (lane >= h_a) & (lane < 2 * h_a)
    is_f = (lane >= 2 * h_a) & (lane < 2 * h_a + h_c)
    g = jnp.where(is_g, g, 0.0)
    lf = jnp.where(is_f, lf, 0.0)
    r = lax.broadcasted_iota(jnp.int32, (tile, tile), 0)
    c = lax.broadcasted_iota(jnp.int32, (tile, tile), 1)
    tri = r >= c
    tril = jnp.where(tri, 1.0, 0.0).astype(F32)
    btril = jnp.where(tri & ((r // DN_CHUNK) == (c // DN_CHUNK)), 1.0, 0.0).astype(F32)
    hi = lax.Precision.HIGHEST
    gcum = jnp.dot(btril, g, precision=hi, preferred_element_type=F32)
    fcum = jnp.dot(tril, lf, precision=hi, preferred_element_type=F32) + carry_ref[0:1, :]
    carry_ref[0:1, :] = fcum[tile - 1:tile, :]
    out = jnp.where(lane < h_a, beta, jnp.where(is_g, gcum, jnp.where(is_f, fcum, 0.0)))
    g_ref[...] = out
    gt_ref[...] = out.T


def _gates(small, a_log, dt_bias, b_f, bsz, seq):
    m = small.shape[0]
    h_a, h_c = a_log.shape[0], b_f.shape[0]
    tile = _tile(seq, 512)
    nt = seq // tile
    p = jnp.zeros((SUBLANES, LANES), F32)
    p = p.at[0, h_a:2 * h_a].set(a_log).at[1, h_a:2 * h_a].set(dt_bias).at[2, 2 * h_a:2 * h_a + h_c].set(b_f)
    return pl.pallas_call(
        functools.partial(_gates_kernel, h_a=h_a, h_c=h_c, tile=tile),
        grid=(bsz, nt),
        in_specs=[pl.BlockSpec((tile, LANES), lambda b, t: (b * nt + t, 0)),
                  pl.BlockSpec((SUBLANES, LANES), lambda b, t: (0, 0))],
        out_specs=[pl.BlockSpec((tile, LANES), lambda b, t: (b * nt + t, 0)),
                   pl.BlockSpec((None, LANES, tile), lambda b, t: (b, 0, t))],
        out_shape=[jax.ShapeDtypeStruct((m, LANES), F32),
                   jax.ShapeDtypeStruct((bsz, LANES, seq), F32)],
        scratch_shapes=[pltpu.VMEM((SUBLANES, LANES), F32)],
        compiler_params=_params(("arbitrary", "arbitrary")),
        name="gates",
    )(small, p)


def _split2(x):
    hi = x.astype(BF16)
    return hi, (x - hi.astype(F32)).astype(BF16)


def _delta_heads(q, k, v, beta, gc, gr, state, masks):
    same, tri, strict, eye = masks
    heads = range(len(q))
    tile = q[0].shape[0]
    decay = [jnp.where(tri, jnp.exp(jnp.where(tri, gc[h] - gr[h], 0.0)), 0.0) for h in heads]
    kb = [k[h] * beta[h] for h in heads]
    k16 = [k[h].astype(BF16) for h in heads]
    kk = [_dot_nt(kb[h].astype(BF16), k16[h]) * decay[h] for h in heads]
    lmat = [jnp.where(strict, kk[h], 0.0) for h in heads]
    xp = [-lmat[h] for h in heads]
    tinv = [eye + xp[h] for h in heads]
    for _ in range(int(math.log2(DN_CHUNK)) - 1):
        xp16 = [xp[h].astype(BF16) for h in heads]
        xp = [_dot(xp16[h], xp16[h]) for h in heads]
        tinv = [tinv[h] + _dot(tinv[h].astype(BF16), xp[h].astype(BF16)) for h in heads]
    tinv16 = [tinv[h].astype(BF16) for h in heads]
    egc = [jnp.exp(gc[h]) for h in heads]
    rhs = [jnp.concatenate([v[h] * beta[h], kb[h] * egc[h]], axis=1) for h in heads]
    uw = [_dot(tinv16[h], rhs[h].astype(BF16)) for h in heads]
    l_hi, l_lo = zip(*[_split2(lmat[h]) for h in heads])
    u_hi, u_lo = zip(*[_split2(uw[h]) for h in heads])
    lu = [_dot(l_hi[h], u_hi[h]) + _dot(l_hi[h], u_lo[h]) + _dot(l_lo[h], u_hi[h]) for h in heads]
    res = [rhs[h] - uw[h] - lu[h] for h in heads]
    uw = [uw[h] + _dot(tinv16[h], res[h].astype(BF16)) for h in heads]
    qk = [_dot_nt(q[h].astype(BF16), k16[h]) * decay[h] for h in heads]
    qe = [q[h] * egc[h] for h in heads]

    state = list(state)
    v_new = [[] for _ in heads]
    o_inter = [[] for _ in heads]
    for i in range(tile // DN_CHUNK):
        lo, hi = i * DN_CHUNK, (i + 1) * DN_CHUNK
        s16 = [state[h].astype(BF16) for h in heads]
        vn = [uw[h][lo:hi, :HEAD_DIM] - _dot(uw[h][lo:hi, HEAD_DIM:].astype(BF16), s16[h]) for h in heads]
        for h in heads:
            o_inter[h].append(_dot(qe[h][lo:hi].astype(BF16), s16[h]))
            v_new[h].append(vn[h])
        g_last = [gc[h][hi - 1:hi, :] for h in heads]
        kd = [k[h][lo:hi] * jnp.exp(g_last[h] - gc[h][lo:hi]) for h in heads]
        state = [state[h] * jnp.exp(g_last[h]) + _dot_tn(kd[h].astype(BF16), vn[h].astype(BF16)) for h in heads]
    o = [jnp.concatenate(o_inter[h], axis=0)
         + _dot(qk[h].astype(BF16), jnp.concatenate(v_new[h], axis=0).astype(BF16)) for h in heads]
    return o, state


def _deltanet_kernel(q_ref, k_ref, v_ref, z_ref, wq_ref, wk_ref, wv_ref, g_ref, gt_ref, na_ref, mix_ref,
                     o_ref, state_ref, tail_ref, *, h_a, hpg, tile, conv_k):
    del mix_ref
    hg = pl.program_id(1)
    t = pl.program_id(2)

    @pl.when(t == 0)
    def _():
        state_ref[...] = jnp.zeros_like(state_ref)
        tail_ref[...] = jnp.zeros_like(tail_ref)

    width = hpg * HEAD_DIM

    def conv_silu(x_ref, w_ref, slot):
        x = x_ref[...].astype(F32)
        w = w_ref[...]
        buf = tail_ref.at[slot]
        buf[SUBLANES:, :] = x
        acc = x * w[conv_k - 1:conv_k, :]
        for sh in range(1, conv_k):
            acc = acc + buf[SUBLANES - sh:SUBLANES - sh + tile, :] * w[conv_k - 1 - sh:conv_k - sh, :]
        buf[:SUBLANES, :] = x[tile - SUBLANES:, :]
        return _silu(acc)

    def l2n(x):
        return x * lax.rsqrt(jnp.sum(x * x, axis=-1, keepdims=True) + RMS_EPS)

    q_all = conv_silu(q_ref, wq_ref, 0)
    k_all = conv_silu(k_ref, wk_ref, 1)
    v_all = conv_silu(v_ref, wv_ref, 2)
    gates = g_ref[...]

    r = lax.broadcasted_iota(jnp.int32, (tile, tile), 0)
    c = lax.broadcasted_iota(jnp.int32, (tile, tile), 1)
    same = (r // DN_CHUNK) == (c // DN_CHUNK)
    masks = (same, same & (r >= c), same & (r > c), jnp.where(r == c, 1.0, 0.0).astype(F32))

    heads = range(hpg)
    cs = [slice(hh * HEAD_DIM, (hh + 1) * HEAD_DIM) for hh in heads]
    q = [l2n(q_all[:, cs[hh]]) * (HEAD_DIM ** -0.5) for hh in heads]
    k = [l2n(k_all[:, cs[hh]]) for hh in heads]
    v = [v_all[:, cs[hh]] for hh in heads]
    beta = [_pick_lane(gates, hg * hpg + hh) for hh in heads]
    gc = [_pick_lane(gates, h_a + hg * hpg + hh) for hh in heads]
    gr = [gt_ref[pl.ds(h_a + hg * hpg + hh, 1), :] for hh in heads]
    gate = _silu(z_ref[...].astype(F32))
    o, state = _delta_heads(q, k, v, beta, gc, gr, [state_ref[hh] for hh in heads], masks)
    for hh in heads:
        state_ref[hh] = state[hh]
        oh = o[hh] * lax.rsqrt(jnp.mean(o[hh] * o[hh], axis=-1, keepdims=True) + RMS_EPS) * na_ref[...]
        o_ref[:, cs[hh]] = (oh * gate[:, cs[hh]]).astype(o_ref.dtype)


def _deltanet(proj, mix, conv_w, gates, gates_t, norm_a, bsz, seq, h_a):
    conv_k = conv_w.shape[0]
    tile = _tile(seq, 256)
    nt = seq // tile
    hpg = next(g for g in (6, 4, 3, 2, 1) if h_a % g == 0)
    ng = h_a // hpg
    width = hpg * HEAD_DIM

    def tok(off):
        return pl.BlockSpec((tile, width), lambda b, g, t: (b * nt + t, off * ng + g))

    def cw(off):
        return pl.BlockSpec((conv_k, width), lambda b, g, t: (0, off * ng + g))

    return pl.pallas_call(
        functools.partial(_deltanet_kernel, h_a=h_a, hpg=hpg, tile=tile, conv_k=conv_k),
        grid=(bsz, ng, nt),
        in_specs=[tok(0), tok(1), tok(2), tok(3), cw(0), cw(1), cw(2),
                  pl.BlockSpec((tile, LANES), lambda b, g, t: (b * nt + t, 0)),
                  pl.BlockSpec((None, LANES, tile), lambda b, g, t: (b, 0, t)),
                  pl.BlockSpec((1, HEAD_DIM), lambda b, g, t: (0, 0)),
                  pl.BlockSpec(memory_space=pl.ANY)],
        out_specs=pl.BlockSpec((tile, width), lambda b, g, t: (b * nt + t, g)),
        out_shape=jax.ShapeDtypeStruct(mix.shape, mix.dtype),
        scratch_shapes=[pltpu.VMEM((hpg, HEAD_DIM, HEAD_DIM), F32),
                        pltpu.VMEM((3, SUBLANES + tile, width), F32)],
        input_output_aliases={10: 0},
        compiler_params=_params(("parallel", "parallel", "arbitrary")),
        name="deltanet",
    )(proj, proj, proj, proj, conv_w, conv_w, conv_w, gates, gates_t, norm_a.reshape(1, HEAD_DIM), mix)


def _gelu(x):
    return 0.5 * x * (1.0 + lax.erf(x * (2.0 ** -0.5)))


def _sgu_kernel(u_ref, vb_ref, v_ref, lg_ref, lb_ref, ws_ref, bs_ref, mix_ref, o_ref, *, chunk, tile, gpb):
    del mix_ref
    jb = pl.program_id(1)
    v = _gelu(v_ref[...].astype(F32))
    mu = jnp.mean(v, axis=-1, keepdims=True)
    vc = v - mu
    rstd = lax.rsqrt(jnp.mean(vc * vc, axis=-1, keepdims=True) + LN_EPS)
    u = _gelu(u_ref[...].astype(F32))
    vn = ((_gelu(vb_ref[...].astype(F32)) - mu) * rstd * lg_ref[...] + lb_ref[...]).astype(BF16)
    r = lax.broadcasted_iota(jnp.int32, (chunk, chunk), 0)
    c = lax.broadcasted_iota(jnp.int32, (chunk, chunk), 1)
    bs = bs_ref[...]
    for gl in range(gpb):
        g = jb * gpb + gl
        w = jnp.where(r >= c, ws_ref[g], 0.0).astype(BF16)
        bias = _pick_lane(bs, g)
        cs = slice(gl * HEAD_DIM, (gl + 1) * HEAD_DIM)
        for n in range(tile // chunk):
            rs = slice(n * chunk, (n + 1) * chunk)
            mixed = _dot(w, vn[rs, cs]) + bias
            o_ref[rs, cs] = (u[rs, cs] * mixed).astype(o_ref.dtype)


def _sgu(proj, mix, col_uv, col_out, d_b, ln_g, ln_b, w_s, b_s):
    m = proj.shape[0]
    groups, chunk, _ = w_s.shape
    tile = _tile(m, 256)
    ob = math.gcd(col_out, d_b)
    assert col_uv % d_b == 0 and ob % HEAD_DIM == 0
    nb = d_b // ob
    return pl.pallas_call(
        functools.partial(_sgu_kernel, chunk=chunk, tile=tile, gpb=ob // HEAD_DIM),
        grid=(m // tile, nb),
        in_specs=[pl.BlockSpec((tile, ob), lambda i, j: (i, col_uv // ob + j)),
                  pl.BlockSpec((tile, ob), lambda i, j: (i, (col_uv + d_b) // ob + j)),
                  pl.BlockSpec((tile, d_b), lambda i, j: (i, col_uv // d_b + 1)),
                  pl.BlockSpec((1, ob), lambda i, j: (0, j)),
                  pl.BlockSpec((1, ob), lambda i, j: (0, j)),
                  pl.BlockSpec((groups, chunk, chunk), lambda i, j: (0, 0, 0)),
                  pl.BlockSpec((chunk, groups), lambda i, j: (0, 0)),
                  pl.BlockSpec(memory_space=pl.ANY)],
        out_specs=pl.BlockSpec((tile, ob), lambda i, j: (i, col_out // ob + j)),
        out_shape=jax.ShapeDtypeStruct(mix.shape, mix.dtype),
        input_output_aliases={7: 0},
        compiler_params=_params(("parallel", "parallel")),
        name="sgu",
    )(proj, proj, proj, ln_g.reshape(1, d_b), ln_b.reshape(1, d_b), w_s, b_s.T, mix)


def _split3(x):
    hi = x.astype(BF16).astype(F32)
    r1 = x - hi
    mid = r1.astype(BF16).astype(F32)
    lo = (r1 - mid).astype(BF16).astype(F32)
    return hi, mid, lo


def _fox_kernel(q_ref, k_ref, v_ref, g_ref, nc_ref, mix_ref, o_ref, kaug_ref, vaug_ref, qaug_ref, m_ref, acc_ref,
                *, lane0, tq, rc, seq, prep):
    del mix_ref
    h = pl.program_id(1)
    qi = pl.program_id(2)

    @pl.when(qi == 0)
    def _():
        lane = lax.broadcasted_iota(jnp.int32, (prep, LANES), 1)
        ones0 = jnp.where(lane == 0, 1.0, 0.0).astype(BF16)
        for r0 in range(0, seq, prep):
            rows = slice(r0, r0 + prep)
            hi, mid, lo = _split3(_pick_lane(g_ref[rows, :], lane0 + h) * LOG2E)
            ex = jnp.where(lane < 3, 1.0,
                           jnp.where(lane == 3, -hi, jnp.where(lane == 4, -mid, jnp.where(lane == 5, -lo, 0.0))))
            kaug_ref[rows, :HEAD_DIM] = k_ref[rows, :]
            kaug_ref[rows, HEAD_DIM:] = ex.astype(BF16)
            vaug_ref[rows, :HEAD_DIM] = v_ref[rows, :]
            vaug_ref[rows, HEAD_DIM:] = ones0

    q0 = pl.multiple_of(qi * tq, tq)
    lane = lax.broadcasted_iota(jnp.int32, (tq, LANES), 1)
    hi, mid, lo = _split3(_pick_lane(g_ref[pl.ds(q0, tq), :], lane0 + h) * LOG2E)
    ex = jnp.where(lane == 0, hi, jnp.where(lane == 1, mid, jnp.where(lane == 2, lo, jnp.where(lane < 6, 1.0, 0.0))))
    qaug_ref[:, :HEAD_DIM] = (q_ref[...].astype(F32) * (HEAD_DIM ** -0.5 * LOG2E)).astype(BF16)
    qaug_ref[:, HEAD_DIM:] = ex.astype(BF16)
    m_ref[...] = jnp.full_like(m_ref, -jnp.inf)
    acc_ref[...] = jnp.zeros_like(acc_ref)

    nch = tq // rc
    row_id = lax.broadcasted_iota(jnp.int32, (rc, LANES), 0)
    col_id = lax.broadcasted_iota(jnp.int32, (rc, LANES), 1)

    def process(start, ncols, diag):
        rows = lambda r: slice(r * rc, (r + 1) * rc)
        scores = [_dot_nt(qaug_ref[rows(r), :], kaug_ref[pl.ds(start, ncols[r]), :]) for r in range(nch)]
        for r in range(nch):
            s, n = scores[r], ncols[r]
            blocks = [s[:, j * LANES:(j + 1) * LANES] for j in range(n // LANES)]
            if diag:
                first = (n - rc) // LANES
                for j in range(first, n // LANES):
                    lo = (j - first) * LANES
                    blocks[j] = jnp.where(col_id + lo <= row_id, blocks[j], -jnp.inf)
            m_prev = m_ref[rows(r), :]
            row_max = functools.reduce(jnp.maximum, blocks)
            m_new = jnp.maximum(m_prev, jnp.max(row_max, axis=-1, keepdims=True))
            p = jnp.concatenate([jnp.exp2(b - m_new) for b in blocks], axis=1).astype(BF16)
            alpha = jnp.exp2(m_prev - m_new)
            acc = acc_ref[rows(r), :]
            scaled = jnp.concatenate([acc[:, :HEAD_DIM] * alpha, acc[:, HEAD_DIM:] * alpha], axis=1)
            acc_ref[rows(r), :] = scaled + _dot(p, vaug_ref[pl.ds(start, n), :])
            m_ref[rows(r), :] = m_new

    def full_square(j, carry):
        process(pl.multiple_of(j * tq, tq), [tq] * nch, False)
        return carry

    lax.fori_loop(0, qi, full_square, 0)
    process(q0, [(r + 1) * rc for r in range(nch)], True)

    acc = acc_ref[...]
    o = acc[:, :HEAD_DIM] / acc[:, HEAD_DIM:HEAD_DIM + 1]
    o = o * lax.rsqrt(jnp.mean(o * o, axis=-1, keepdims=True) + RMS_EPS) * nc_ref[...]
    o_ref[...] = o.astype(o_ref.dtype)


def _fox(proj, mix, col_in, col_out, gates, norm_c, bsz, seq, h_c, lane0):
    tq = _tile(seq, 2048)
    rc = _tile(tq, 256)
    nq = seq // tq
    blk_in, blk_out = col_in // HEAD_DIM, col_out // HEAD_DIM
    prep = _tile(seq, 1024)
    return pl.pallas_call(
        functools.partial(_fox_kernel, lane0=lane0, tq=tq, rc=rc, seq=seq, prep=prep),
        grid=(bsz, h_c, nq),
        in_specs=[pl.BlockSpec((tq, HEAD_DIM), lambda b, h, qi: (b * nq + qi, blk_in + h)),
                  pl.BlockSpec((seq, HEAD_DIM), lambda b, h, qi: (b, blk_in + h_c + h)),
                  pl.BlockSpec((seq, HEAD_DIM), lambda b, h, qi: (b, blk_in + 2 * h_c + h)),
                  pl.BlockSpec((seq, LANES), lambda b, h, qi: (b, 0)),
                  pl.BlockSpec((1, HEAD_DIM), lambda b, h, qi: (0, 0)),
                  pl.BlockSpec(memory_space=pl.ANY)],
        out_specs=pl.BlockSpec((tq, HEAD_DIM), lambda b, h, qi: (b * nq + qi, blk_out + h)),
        out_shape=jax.ShapeDtypeStruct(mix.shape, mix.dtype),
        scratch_shapes=[pltpu.VMEM((seq, 2 * HEAD_DIM), BF16), pltpu.VMEM((seq, 2 * HEAD_DIM), BF16),
                        pltpu.VMEM((tq, 2 * HEAD_DIM), BF16), pltpu.VMEM((tq, LANES), F32),
                        pltpu.VMEM((tq, 2 * HEAD_DIM), F32)],
        input_output_aliases={5: 0},
        compiler_params=_params(("parallel", "parallel", "arbitrary")),
        name="fox_attention",
    )(proj, proj, proj, gates, norm_c.reshape(1, HEAD_DIM), mix)


def kernel(x, c, w_ada, b_ada, ln_g, ln_b, w_ffn_in, w_ffn_out, w_in, conv_w, a_log, dt_bias, norm_a,
           sgu_ln_g, sgu_ln_b, w_s, b_s, b_f, norm_c, w_o):
    bsz, seq, d = x.shape
    depth = w_ada.shape[0]
    m = bsz * seq
    h_a, h_c = a_log.shape[1], b_f.shape[1]
    d_a, d_c = h_a * HEAD_DIM, h_c * HEAD_DIM
    d_b = sgu_ln_g.shape[1]
    alpha = (2.0 * depth) ** 0.25
    assert 2 * h_a + h_c <= LANES and seq % DN_CHUNK == 0 and seq % w_s.shape[-1] == 0
    assert w_in.shape[-1] == 4 * d_a + 2 * h_a + 2 * d_b + 3 * d_c + h_c and d_a + d_b + d_c == d

    c_pad = jnp.zeros((SUBLANES, d), F32).at[:bsz].set(c)
    mod = _ada(c_pad, w_ada, b_ada)[:, :bsz]

    def mods(l):
        return [v.reshape(bsz, 1, d) for v in jnp.split(mod[l], 9, axis=-1)]

    w_big, w_small = _win_prep(w_in, d_a, d_b, d_c, h_a, h_c)
    w_ffn_in16 = w_ffn_in.astype(BF16)
    w_ffn_out16 = w_ffn_out.astype(BF16)
    w_o16 = w_o.astype(BF16)
    col_uv = 4 * d_a
    col_c = col_uv + 2 * d_b

    x2 = x.reshape(m, d)
    h = None
    for l in range(depth):
        sh1, sc1, ga1, sh2, sc2, ga2, sh3, sc3, ga3 = mods(l)
        if l == 0:
            h = _modulate(x2, sh1, sc1, seq)

        act = _swiglu_in(h, w_ffn_in16, (l, 0))
        y = _matmul_ktiled(act, w_ffn_out16, (l, 0), BF16, name="ffn_out")
        x2, h = _resid_ln(x2, y, ga1, ln_g[l, 0], ln_b[l, 0], (sh2, sc2), seq, alpha, 0.5)

        proj = _matmul(h, w_big, (l,), BF16, tn=1280, name="mixer_in")
        small = _matmul(h, w_small, (l,), F32, tn=LANES, name="mixer_in_small")
        gates, gates_t = _gates(small, a_log[l], dt_bias[l], b_f[l], bsz, seq)
        mix = jnp.zeros((m, d), BF16)
        mix = _deltanet(proj, mix, conv_w[l], gates, gates_t, norm_a[l], bsz, seq, h_a)
        mix = _sgu(proj, mix, col_uv, d_a, d_b, sgu_ln_g[l], sgu_ln_b[l], w_s[l], b_s[l])
        mix = _fox(proj, mix, col_c, d_a + d_b, gates, norm_c[l], bsz, seq, h_c, 2 * h_a)
        y = _matmul(mix, w_o16, (l,), BF16, name="mixer_out")
        x2, h = _resid_ln(x2, y, ga2, ln_g[l, 1], ln_b[l, 1], (sh3, sc3), seq, alpha, 1.0)

        act = _swiglu_in(h, w_ffn_in16, (l, 1))
        y = _matmul_ktiled(act, w_ffn_out16, (l, 1), BF16, name="ffn_out")
        nxt = None
        if l + 1 < depth:
            nsh, nsc = mods(l + 1)[:2]
            nxt = (nsh, nsc)
        x2, h = _resid_ln(x2, y, ga3, ln_g[l, 2], ln_b[l, 2], nxt, seq, alpha, 0.5)
    return x2.reshape(bsz, seq, d)
```

```python
import functools
import math

import jax
import jax.numpy as jnp
from jax import lax
from jax.experimental import pallas as pl
from jax.experimental.pallas import tpu as pltpu

F32 = jnp.float32
BF16 = jnp.bfloat16

HEAD_DIM = 128
DN_CHUNK = 64
LN_EPS = 1e-5
RMS_EPS = 1e-6
LOG2E = 1.4426950408889634
LANES = 128
SUBLANES = 8
VMEM_LIMIT = 56 * 1024 * 1024


def _params(sem, vmem=VMEM_LIMIT):
    return pltpu.CompilerParams(dimension_semantics=sem, vmem_limit_bytes=vmem)


def _tile(n, pref):
    t = min(n, pref)
    while n % t:
        t //= 2
    return t


def _silu(x):
    return x * jax.nn.sigmoid(x)


def _softplus(x):
    return jnp.maximum(x, 0.0) + jnp.log1p(jnp.exp(-jnp.abs(x)))


def _dot(a, b):
    return jnp.dot(a, b, preferred_element_type=F32)


def _dot_nt(a, b):
    return lax.dot_general(a, b, (((1,), (1,)), ((), ())), preferred_element_type=F32)


def _dot_tn(a, b):
    return lax.dot_general(a, b, (((0,), (0,)), ((), ())), preferred_element_type=F32)


def _pick_lane(x, idx):
    lane = lax.broadcasted_iota(jnp.int32, x.shape, 1)
    return jnp.sum(jnp.where(lane == idx, x, 0.0), axis=-1, keepdims=True)


def _ada_kernel(c_ref, w_ref, b_ref, o_ref):
    c = c_ref[...]
    o_ref[...] = _dot(_silu(c).astype(BF16), w_ref[...].astype(BF16)) + b_ref[...]


def _ada(c_pad, w_ada, b_ada):
    depth, d, n = w_ada.shape
    bp = c_pad.shape[0]
    tn = _tile(n, 512)
    return pl.pallas_call(
        _ada_kernel,
        grid=(depth, n // tn),
        in_specs=[
            pl.BlockSpec((bp, d), lambda l, j: (0, 0)),
            pl.BlockSpec((None, d, tn), lambda l, j: (l, 0, j)),
            pl.BlockSpec((None, 1, tn), lambda l, j: (l, 0, j)),
        ],
        out_specs=pl.BlockSpec((None, bp, tn), lambda l, j: (l, 0, j)),
        out_shape=jax.ShapeDtypeStruct((depth, bp, n), F32),
        compiler_params=_params(("parallel", "parallel")),
        name="ada_mod",
    )(c_pad, w_ada, b_ada.reshape(depth, 1, n))


def _modulate_kernel(x_ref, sh_ref, sc_ref, h_ref):
    h_ref[...] = (x_ref[...] * (1.0 + sc_ref[...]) + sh_ref[...]).astype(h_ref.dtype)


def _modulate(x2, sh, sc, seq):
    m, d = x2.shape
    tm = _tile(seq, 512)
    per = seq // tm
    vec = pl.BlockSpec((None, 1, d), lambda i: (i // per, 0, 0))
    return pl.pallas_call(
        _modulate_kernel,
        grid=(m // tm,),
        in_specs=[pl.BlockSpec((tm, d), lambda i: (i, 0)), vec, vec],
        out_specs=pl.BlockSpec((tm, d), lambda i: (i, 0)),
        out_shape=jax.ShapeDtypeStruct((m, d), BF16),
        compiler_params=_params(("parallel",)),
        name="modulate",
    )(x2, sh, sc)


def _win_big_kernel(lo_ref, hi_ref, o_ref, *, n_aligned, group, keep):
    j = pl.program_id(1)

    @pl.when(j < n_aligned)
    def _():
        o_ref[...] = lo_ref[...].astype(o_ref.dtype)

    @pl.when(j >= n_aligned)
    def _():
        lane = lax.broadcasted_iota(jnp.int32, (lo_ref.shape[0], LANES), 1)
        for b in range(group):
            lo = lo_ref[:, b * LANES:(b + 1) * LANES]
            hi = lo_ref[:, (b + 1) * LANES:(b + 2) * LANES] if b + 1 < group else hi_ref[...]
            shifted = jnp.where(lane < keep, pltpu.roll(lo, keep, 1), pltpu.roll(hi, keep, 1))
            o_ref[:, b * LANES:(b + 1) * LANES] = shifted.astype(o_ref.dtype)


def _win_small_kernel(a_ref, b_ref, o_ref, *, n0, n1):
    lane = lax.broadcasted_iota(jnp.int32, a_ref.shape, 1)
    o_ref[...] = jnp.where(lane < n0, a_ref[...], jnp.where(lane < n0 + n1, b_ref[...], 0.0)).astype(o_ref.dtype)


def _win_prep(w_in, d_a, d_b, d_c, h_a, h_c):
    depth, d, _ = w_in.shape
    n_al = 4 * d_a // LANES
    n_sh = (2 * d_b + 3 * d_c) // LANES
    shift = 2 * h_a
    group = math.gcd(math.gcd(n_al, n_sh), 4)
    cw = group * LANES
    big = pl.pallas_call(
        functools.partial(_win_big_kernel, n_aligned=n_al // group, group=group, keep=LANES - shift),
        grid=(depth, (n_al + n_sh) // group),
        in_specs=[pl.BlockSpec((None, d, cw), lambda l, j: (l, 0, j)),
                  pl.BlockSpec((None, d, LANES), lambda l, j: (l, 0, (j + 1) * group))],
        out_specs=pl.BlockSpec((None, d, cw), lambda l, j: (l, 0, j)),
        out_shape=jax.ShapeDtypeStruct((depth, d, (n_al + n_sh) * LANES), BF16),
        compiler_params=_params(("parallel", "parallel")),
        name="w_in_relayout",
    )(w_in, w_in)
    small = pl.pallas_call(
        functools.partial(_win_small_kernel, n0=shift, n1=h_c),
        grid=(depth,),
        in_specs=[pl.BlockSpec((None, d, LANES), lambda l: (l, 0, n_al)),
                  pl.BlockSpec((None, d, LANES), lambda l: (l, 0, n_al + n_sh))],
        out_specs=pl.BlockSpec((None, d, LANES), lambda l: (l, 0, 0)),
        out_shape=jax.ShapeDtypeStruct((depth, d, LANES), BF16),
        compiler_params=_params(("parallel",)),
        name="w_in_small",
    )(w_in, w_in)
    return big, small


def _mm_kernel(a_ref, w_ref, o_ref):
    o_ref[...] = _dot(a_ref[...], w_ref[...]).astype(o_ref.dtype)


def _wspec(lead, rows, cols, index):
    return pl.BlockSpec((None,) * len(lead) + (rows, cols), lambda *g: tuple(lead) + tuple(index(*g)))


def _matmul(a, w, lead, out_dtype, tm=1024, tn=1024, name="matmul"):
    m, k = a.shape
    n = w.shape[-1]
    tm, tn = _tile(m, tm), _tile(n, tn)
    return pl.pallas_call(
        _mm_kernel,
        grid=(m // tm, n // tn),
        in_specs=[pl.BlockSpec((tm, k), lambda i, j: (i, 0)),
                  _wspec(lead, k, tn, lambda i, j: (0, j))],
        out_specs=pl.BlockSpec((tm, tn), lambda i, j: (i, j)),
        out_shape=jax.ShapeDtypeStruct((m, n), out_dtype),
        compiler_params=_params(("parallel", "parallel")),
        name=name,
    )(a, w)


def _swiglu_kernel(a_ref, wg_ref, wu_ref, o_ref):
    a = a_ref[...]
    g = _dot(a, wg_ref[...])
    u = _dot(a, wu_ref[...])
    o_ref[...] = (_silu(g) * u).astype(o_ref.dtype)


def _swiglu_in(h, w_in, lead, tm=1024, tf=512):
    m, k = h.shape
    f = w_in.shape[-1] // 2
    tm, tf = _tile(m, tm), _tile(f, tf)
    nf = f // tf
    return pl.pallas_call(
        _swiglu_kernel,
        grid=(m // tm, nf),
        in_specs=[pl.BlockSpec((tm, k), lambda i, j: (i, 0)),
                  _wspec(lead, k, tf, lambda i, j: (0, j)),
                  _wspec(lead, k, tf, lambda i, j: (0, j + nf))],
        out_specs=pl.BlockSpec((tm, tf), lambda i, j: (i, j)),
        out_shape=jax.ShapeDtypeStruct((m, f), BF16),
        compiler_params=_params(("parallel", "parallel")),
        name="ffn_in_swiglu",
    )(h, w_in, w_in)


def _mm_acc_kernel(a_ref, w_ref, o_ref, acc_ref):
    kk = pl.program_id(2)

    @pl.when(kk == 0)
    def _():
        acc_ref[...] = jnp.zeros_like(acc_ref)

    acc_ref[...] += _dot(a_ref[...], w_ref[...])

    @pl.when(kk == pl.num_programs(2) - 1)
    def _():
        o_ref[...] = acc_ref[...].astype(o_ref.dtype)


def _matmul_ktiled(a, w, lead, out_dtype, tm=1024, tn=1024, tk=4096, name="matmul_k"):
    m, k = a.shape
    n = w.shape[-1]
    tm, tn, tk = _tile(m, tm), _tile(n, tn), _tile(k, tk)
    return pl.pallas_call(
        _mm_acc_kernel,
        grid=(m // tm, n // tn, k // tk),
        in_specs=[pl.BlockSpec((tm, tk), lambda i, j, q: (i, q)),
                  _wspec(lead, tk, tn, lambda i, j, q: (q, j))],
        out_specs=pl.BlockSpec((tm, tn), lambda i, j, q: (i, j)),
        out_shape=jax.ShapeDtypeStruct((m, n), out_dtype),
        scratch_shapes=[pltpu.VMEM((tm, tn), F32)],
        compiler_params=_params(("parallel", "parallel", "arbitrary")),
        name=name,
    )(a, w)


def _resid_ln_kernel(x_ref, y_ref, ga_ref, g_ref, b_ref, *rest, alpha, coef, with_next):
    if with_next:
        sh_ref, sc_ref, xo_ref, h_ref = rest
    else:
        (xo_ref,) = rest
    r = alpha * x_ref[...] + (coef * ga_ref[...]) * y_ref[...].astype(F32)
    mu = jnp.mean(r, axis=-1, keepdims=True)
    rc = r - mu
    var = jnp.mean(rc * rc, axis=-1, keepdims=True)
    xn = rc * lax.rsqrt(var + LN_EPS) * g_ref[...] + b_ref[...]
    xo_ref[...] = xn
    if with_next:
        h_ref[...] = (xn * (1.0 + sc_ref[...]) + sh_ref[...]).astype(h_ref.dtype)


def _resid_ln(x2, y, ga, ln_g, ln_b, nxt, seq, alpha, coef):
    m, d = x2.shape
    tm = _tile(seq, 256)
    per = seq // tm
    row = pl.BlockSpec((tm, d), lambda i: (i, 0))
    vec = pl.BlockSpec((None, 1, d), lambda i: (i // per, 0, 0))
    par = pl.BlockSpec((1, d), lambda i: (0, 0))
    with_next = nxt is not None
    in_specs = [row, row, vec, par, par]
    args = [x2, y, ga, ln_g.reshape(1, d), ln_b.reshape(1, d)]
    out_specs = [row]
    out_shape = [jax.ShapeDtypeStruct((m, d), F32)]
    if with_next:
        in_specs += [vec, vec]
        args += list(nxt)
        out_specs.append(row)
        out_shape.append(jax.ShapeDtypeStruct((m, d), BF16))
    res = pl.pallas_call(
        functools.partial(_resid_ln_kernel, alpha=alpha, coef=coef, with_next=with_next),
        grid=(m // tm,),
        in_specs=in_specs,
        out_specs=out_specs,
        out_shape=out_shape,
        compiler_params=_params(("parallel",)),
        name="resid_ln",
    )(*args)
    return (res[0], res[1]) if with_next else (res[0], None)


def _gates_kernel(s_ref, p_ref, g_ref, gt_ref, carry_ref, *, h_a, h_c, tile):
    t = pl.program_id(1)

    @pl.when(t == 0)
    def _():
        carry_ref[...] = jnp.zeros_like(carry_ref)

    x = s_ref[...]
    a_log = p_ref[0:1, :]
    dt_b = p_ref[1:2, :]
    b_f = p_ref[2:3, :]
    lane = lax.broadcasted_iota(jnp.int32, x.shape, 1)
    beta = jax.nn.sigmoid(x)
    g = -jnp.exp(a_log) * _softplus(x + dt_b)
    lf = -_softplus(-(x + b_f))
    is_g = (lane >= h_a) & (lane < 2 * h_a)
    is_f = (lane >= 2 * h_a) & (lane < 2 * h_a + h_c)
    g = jnp.where(is_g, g, 0.0)
    lf = jnp.where(is_f, lf, 0.0)
    r = lax.broadcasted_iota(jnp.int32, (tile, tile), 0)
    c = lax.broadcasted_iota(jnp.int32, (tile, tile), 1)
    tri = r >= c
    tril = jnp.where(tri, 1.0, 0.0).astype(F32)
    btril = jnp.where(tri & ((r // DN_CHUNK) == (c // DN_CHUNK)), 1.0, 0.0).astype(F32)
    hi = lax.Precision.HIGHEST
    gcum = jnp.dot(btril, g, precision=hi, preferred_element_type=F32)
    fcum = jnp.dot(tril, lf, precision=hi, preferred_element_type=F32) + carry_ref[0:1, :]
    carry_ref[0:1, :] = fcum[tile - 1:tile, :]
    out = jnp.where(lane < h_a, beta, jnp.where(is_g, gcum, jnp.where(is_f, fcum, 0.0)))
    g_ref[...] = out
    gt_ref[...] = out.T


def _gates(small, a_log, dt_bias, b_f, bsz, seq):
    m = small.shape[0]
    h_a, h_c = a_log.shape[0], b_f.shape[0]
    tile = _tile(seq, 512)
    nt = seq // tile
    p = jnp.zeros((SUBLANES, LANES), F32)
    p = p.at[0, h_a:2 * h_a].set(a_log).at[1, h_a:2 * h_a].set(dt_bias).at[2, 2 * h_a:2 * h_a + h_c].set(b_f)
    return pl.pallas_call(
        functools.partial(_gates_kernel, h_a=h_a, h_c=h_c, tile=tile),
        grid=(bsz, nt),
        in_specs=[pl.BlockSpec((tile, LANES), lambda b, t: (b * nt + t, 0)),
                  pl.BlockSpec((SUBLANES, LANES), lambda b, t: (0, 0))],
        out_specs=[pl.BlockSpec((tile, LANES), lambda b, t: (b * nt + t, 0)),
                   pl.BlockSpec((None, LANES, tile), lambda b, t: (b, 0, t))],
        out_shape=[jax.ShapeDtypeStruct((m, LANES), F32),
                   jax.ShapeDtypeStruct((bsz, LANES, seq), F32)],
        scratch_shapes=[pltpu.VMEM((SUBLANES, LANES), F32)],
        compiler_params=_params(("arbitrary", "arbitrary")),
        name="gates",
    )(small, p)


def _split2(x):
    hi = x.astype(BF16)
    return hi, (x - hi.astype(F32)).astype(BF16)


def _delta_heads(q, k, v, beta, gc, gr, state, masks):
    same, tri, strict, eye = masks
    heads = range(len(q))
    tile = q[0].shape[0]
    decay = [jnp.where(tri, jnp.exp(jnp.where(tri, gc[h] - gr[h], 0.0)), 0.0) for h in heads]
    kb = [k[h] * beta[h] for h in heads]
    k16 = [k[h].astype(BF16) for h in heads]
    kk = [_dot_nt(kb[h].astype(BF16), k16[h]) * decay[h] for h in heads]
    lmat = [jnp.where(strict, kk[h], 0.0) for h in heads]
    xp = [-lmat[h] for h in heads]
    tinv = [eye + xp[h] for h in heads]
    for _ in range(int(math.log2(DN_CHUNK)) - 1):
        xp16 = [xp[h].astype(BF16) for h in heads]
        xp = [_dot(xp16[h], xp16[h]) for h in heads]
        tinv = [tinv[h] + _dot(tinv[h].astype(BF16), xp[h].astype(BF16)) for h in heads]
    tinv16 = [tinv[h].astype(BF16) for h in heads]
    egc = [jnp.exp(gc[h]) for h in heads]
    rhs = [jnp.concatenate([v[h] * beta[h], kb[h] * egc[h]], axis=1) for h in heads]
    uw = [_dot(tinv16[h], rhs[h].astype(BF16)) for h in heads]
    l_hi, l_lo = zip(*[_split2(lmat[h]) for h in heads])
    u_hi, u_lo = zip(*[_split2(uw[h]) for h in heads])
    lu = [_dot(l_hi[h], u_hi[h]) + _dot(l_hi[h], u_lo[h]) + _dot(l_lo[h], u_hi[h]) for h in heads]
    res = [rhs[h] - uw[h] - lu[h] for h in heads]
    uw = [uw[h] + _dot(tinv16[h], res[h].astype(BF16)) for h in heads]
    qk = [_dot_nt(q[h].astype(BF16), k16[h]) * decay[h] for h in heads]
    qe = [q[h] * egc[h] for h in heads]

    state = list(state)
    v_new = [[] for _ in heads]
    o_inter = [[] for _ in heads]
    for i in range(tile // DN_CHUNK):
        lo, hi = i * DN_CHUNK, (i + 1) * DN_CHUNK
        s16 = [state[h].astype(BF16) for h in heads]
        vn = [uw[h][lo:hi, :HEAD_DIM] - _dot(uw[h][lo:hi, HEAD_DIM:].astype(BF16), s16[h]) for h in heads]
        for h in heads:
            o_inter[h].append(_dot(qe[h][lo:hi].astype(BF16), s16[h]))
            v_new[h].append(vn[h])
        g_last = [gc[h][hi - 1:hi, :] for h in heads]
        kd = [k[h][lo:hi] * jnp.exp(g_last[h] - gc[h][lo:hi]) for h in heads]
        state = [state[h] * jnp.exp(g_last[h]) + _dot_tn(kd[h].astype(BF16), vn[h].astype(BF16)) for h in heads]
    o = [jnp.concatenate(o_inter[h], axis=0)
         + _dot(qk[h].astype(BF16), jnp.concatenate(v_new[h], axis=0).astype(BF16)) for h in heads]
    return o, state


def _deltanet_kernel(q_ref, k_ref, v_ref, z_ref, wq_ref, wk_ref, wv_ref, g_ref, gt_ref, na_ref, mix_ref,
                     o_ref, state_ref, tail_ref, *, h_a, hpg, tile, conv_k):
    del mix_ref
    hg = pl.program_id(1)
    t = pl.program_id(2)

    @pl.when(t == 0)
    def _():
        state_ref[...] = jnp.zeros_like(state_ref)
        tail_ref[...] = jnp.zeros_like(tail_ref)

    width = hpg * HEAD_DIM

    def conv_silu(x_ref, w_ref, slot):
        x = x_ref[...].astype(F32)
        w = w_ref[...]
        buf = tail_ref.at[slot]
        buf[SUBLANES:, :] = x
        acc = x * w[conv_k - 1:conv_k, :]
        for sh in range(1, conv_k):
            acc = acc + buf[SUBLANES - sh:SUBLANES - sh + tile, :] * w[conv_k - 1 - sh:conv_k - sh, :]
        buf[:SUBLANES, :] = x[tile - SUBLANES:, :]
        return _silu(acc)

    def l2n(x):
        return x * lax.rsqrt(jnp.sum(x * x, axis=-1, keepdims=True) + RMS_EPS)

    q_all = conv_silu(q_ref, wq_ref, 0)
    k_all = conv_silu(k_ref, wk_ref, 1)
    v_all = conv_silu(v_ref, wv_ref, 2)
    gates = g_ref[...]

    r = lax.broadcasted_iota(jnp.int32, (tile, tile), 0)
    c = lax.broadcasted_iota(jnp.int32, (tile, tile), 1)
    same = (r // DN_CHUNK) == (c // DN_CHUNK)
    masks = (same, same & (r >= c), same & (r > c), jnp.where(r == c, 1.0, 0.0).astype(F32))

    heads = range(hpg)
    cs = [slice(hh * HEAD_DIM, (hh + 1) * HEAD_DIM) for hh in heads]
    q = [l2n(q_all[:, cs[hh]]) * (HEAD_DIM ** -0.5) for hh in heads]
    k = [l2n(k_all[:, cs[hh]]) for hh in heads]
    v = [v_all[:, cs[hh]] for hh in heads]
    beta = [_pick_lane(gates, hg * hpg + hh) for hh in heads]
    gc = [_pick_lane(gates, h_a + hg * hpg + hh) for hh in heads]
    gr = [gt_ref[pl.ds(h_a + hg * hpg + hh, 1), :] for hh in heads]
    gate = _silu(z_ref[...].astype(F32))
    o, state = _delta_heads(q, k, v, beta, gc, gr, [state_ref[hh] for hh in heads], masks)
    for hh in heads:
        state_ref[hh] = state[hh]
        oh = o[hh] * lax.rsqrt(jnp.mean(o[hh] * o[hh], axis=-1, keepdims=True) + RMS_EPS) * na_ref[...]
        o_ref[:, cs[hh]] = (oh * gate[:, cs[hh]]).astype(o_ref.dtype)


def _deltanet(proj, mix, conv_w, gates, gates_t, norm_a, bsz, seq, h_a):
    conv_k = conv_w.shape[0]
    tile = _tile(seq, 256)
    nt = seq // tile
    hpg = next(g for g in (12, 6, 4, 3, 2, 1) if h_a % g == 0)
    ng = h_a // hpg
    width = hpg * HEAD_DIM

    def tok(off):
        return pl.BlockSpec((tile, width), lambda b, g, t: (b * nt + t, off * ng + g))

    def cw(off):
        return pl.BlockSpec((conv_k, width), lambda b, g, t: (0, off * ng + g))

    return pl.pallas_call(
        functools.partial(_deltanet_kernel, h_a=h_a, hpg=hpg, tile=tile, conv_k=conv_k),
        grid=(bsz, ng, nt),
        in_specs=[tok(0), tok(1), tok(2), tok(3), cw(0), cw(1), cw(2),
                  pl.BlockSpec((tile, LANES), lambda b, g, t: (b * nt + t, 0)),
                  pl.BlockSpec((None, LANES, tile), lambda b, g, t: (b, 0, t)),
                  pl.BlockSpec((1, HEAD_DIM), lambda b, g, t: (0, 0)),
                  pl.BlockSpec(memory_space=pl.ANY)],
        out_specs=pl.BlockSpec((tile, width), lambda b, g, t: (b * nt + t, g)),
        out_shape=jax.ShapeDtypeStruct(mix.shape, mix.dtype),
        scratch_shapes=[pltpu.VMEM((hpg, HEAD_DIM, HEAD_DIM), F32),
                        pltpu.VMEM((3, SUBLANES + tile, width), F32)],
        input_output_aliases={10: 0},
        compiler_params=_params(("parallel", "parallel", "arbitrary")),
        name="deltanet",
    )(proj, proj, proj, proj, conv_w, conv_w, conv_w, gates, gates_t, norm_a.reshape(1, HEAD_DIM), mix)


def _gelu(x):
    return 0.5 * x * (1.0 + lax.erf(x * (2.0 ** -0.5)))


def _sgu_kernel(u_ref, vb_ref, v_ref, lg_ref, lb_ref, ws_ref, bs_ref, mix_ref, o_ref, *, chunk, tile, gpb):
    del mix_ref
    jb = pl.program_id(1)
    v = _gelu(v_ref[...].astype(F32))
    mu = jnp.mean(v, axis=-1, keepdims=True)
    vc = v - mu
    rstd = lax.rsqrt(jnp.mean(vc * vc, axis=-1, keepdims=True) + LN_EPS)
    u = _gelu(u_ref[...].astype(F32))
    vn = ((_gelu(vb_ref[...].astype(F32)) - mu) * rstd * lg_ref[...] + lb_ref[...]).astype(BF16)
    r = lax.broadcasted_iota(jnp.int32, (chunk, chunk), 0)
    c = lax.broadcasted_iota(jnp.int32, (chunk, chunk), 1)
    bs = bs_ref[...]
    for gl in range(gpb):
        g = jb * gpb + gl
        w = jnp.where(r >= c, ws_ref[g], 0.0).astype(BF16)
        bias = _pick_lane(bs, g)
        cs = slice(gl * HEAD_DIM, (gl + 1) * HEAD_DIM)
        for n in range(tile // chunk):
            rs = slice(n * chunk, (n + 1) * chunk)
            mixed = _dot(w, vn[rs, cs]) + bias
            o_ref[rs, cs] = (u[rs, cs] * mixed).astype(o_ref.dtype)


def _sgu(proj, mix, col_uv, col_out, d_b, ln_g, ln_b, w_s, b_s):
    m = proj.shape[0]
    groups, chunk, _ = w_s.shape
    tile = _tile(m, 256)
    ob = math.gcd(col_out, d_b)
    assert col_uv % d_b == 0 and ob % HEAD_DIM == 0
    nb = d_b // ob
    return pl.pallas_call(
        functools.partial(_sgu_kernel, chunk=chunk, tile=tile, gpb=ob // HEAD_DIM),
        grid=(m // tile, nb),
        in_specs=[pl.BlockSpec((tile, ob), lambda i, j: (i, col_uv // ob + j)),
                  pl.BlockSpec((tile, ob), lambda i, j: (i, (col_uv + d_b) // ob + j)),
                  pl.BlockSpec((tile, d_b), lambda i, j: (i, col_uv // d_b + 1)),
                  pl.BlockSpec((1, ob), lambda i, j: (0, j)),
                  pl.BlockSpec((1, ob), lambda i, j: (0, j)),
                  pl.BlockSpec((groups, chunk, chunk), lambda i, j: (0, 0, 0)),
                  pl.BlockSpec((chunk, groups), lambda i, j: (0, 0)),
                  pl.BlockSpec(memory_space=pl.ANY)],
        out_specs=pl.BlockSpec((tile, ob), lambda i, j: (i, col_out // ob + j)),
        out_shape=jax.ShapeDtypeStruct(mix.shape, mix.dtype),
        input_output_aliases={7: 0},
        compiler_params=_params(("parallel", "parallel")),
        name="sgu",
    )(proj, proj, proj, ln_g.reshape(1, d_b), ln_b.reshape(1, d_b), w_s, b_s.T, mix)


def _split3(x):
    hi = x.astype(BF16).astype(F32)
    r1 = x - hi
    mid = r1.astype(BF16).astype(F32)
    lo = (r1 - mid).astype(BF16).astype(F32)
    return hi, mid, lo


def _fox_kernel(q_ref, k_ref, v_ref, g_ref, nc_ref, mix_ref, o_ref, kaug_ref, vaug_ref, qaug_ref, m_ref, acc_ref,
                *, lane0, tq, rc, seq, prep):
    del mix_ref
    h = pl.program_id(1)
    qi = pl.program_id(2)

    @pl.when(qi == 0)
    def _():
        lane = lax.broadcasted_iota(jnp.int32, (prep, LANES), 1)
        ones0 = jnp.where(lane == 0, 1.0, 0.0).astype(BF16)
        for r0 in range(0, seq, prep):
            rows = slice(r0, r0 + prep)
            hi, mid, lo = _split3(_pick_lane(g_ref[rows, :], lane0 + h) * LOG2E)
            ex = jnp.where(lane < 3, 1.0,
                           jnp.where(lane == 3, -hi, jnp.where(lane == 4, -mid, jnp.where(lane == 5, -lo, 0.0))))
            kaug_ref[rows, :HEAD_DIM] = k_ref[rows, :]
            kaug_ref[rows, HEAD_DIM:] = ex.astype(BF16)
            vaug_ref[rows, :HEAD_DIM] = v_ref[rows, :]
            vaug_ref[rows, HEAD_DIM:] = ones0

    q0 = pl.multiple_of(qi * tq, tq)
    lane = lax.broadcasted_iota(jnp.int32, (tq, LANES), 1)
    hi, mid, lo = _split3(_pick_lane(g_ref[pl.ds(q0, tq), :], lane0 + h) * LOG2E)
    ex = jnp.where(lane == 0, hi, jnp.where(lane == 1, mid, jnp.where(lane == 2, lo, jnp.where(lane < 6, 1.0, 0.0))))
    qaug_ref[:, :HEAD_DIM] = (q_ref[...].astype(F32) * (HEAD_DIM ** -0.5 * LOG2E)).astype(BF16)
    qaug_ref[:, HEAD_DIM:] = ex.astype(BF16)
    m_ref[...] = jnp.full_like(m_ref, -jnp.inf)
    acc_ref[...] = jnp.zeros_like(acc_ref)

    nch = tq // rc
    row_id = lax.broadcasted_iota(jnp.int32, (rc, LANES), 0)
    col_id = lax.broadcasted_iota(jnp.int32, (rc, LANES), 1)

    def process(start, ncols, diag):
        rows = lambda r: slice(r * rc, (r + 1) * rc)
        scores = [_dot_nt(qaug_ref[rows(r), :], kaug_ref[pl.ds(start, ncols[r]), :]) for r in range(nch)]
        for r in range(nch):
            s, n = scores[r], ncols[r]
            blocks = [s[:, j * LANES:(j + 1) * LANES] for j in range(n // LANES)]
            if diag:
                first = (n - rc) // LANES
                for j in range(first, n // LANES):
                    lo = (j - first) * LANES
                    blocks[j] = jnp.where(col_id + lo <= row_id, blocks[j], -jnp.inf)
            m_prev = m_ref[rows(r), :]
            row_max = functools.reduce(jnp.maximum, blocks)
            m_new = jnp.maximum(m_prev, jnp.max(row_max, axis=-1, keepdims=True))
            p = jnp.concatenate([jnp.exp2(b - m_new) for b in blocks], axis=1).astype(BF16)
            alpha = jnp.exp2(m_prev - m_new)
            acc = acc_ref[rows(r), :]
            scaled = jnp.concatenate([acc[:, :HEAD_DIM] * alpha, acc[:, HEAD_DIM:] * alpha], axis=1)
            acc_ref[rows(r), :] = scaled + _dot(p, vaug_ref[pl.ds(start, n), :])
            m_ref[rows(r), :] = m_new

    def full_square(j, carry):
        process(pl.multiple_of(j * tq, tq), [tq] * nch, False)
        return carry

    lax.fori_loop(0, qi, full_square, 0)
    process(q0, [(r + 1) * rc for r in range(nch)], True)

    acc = acc_ref[...]
    o = acc[:, :HEAD_DIM] / acc[:, HEAD_DIM:HEAD_DIM + 1]
    o = o * lax.rsqrt(jnp.mean(o * o, axis=-1, keepdims=True) + RMS_EPS) * nc_ref[...]
    o_ref[...] = o.astype(o_ref.dtype)


def _fox(proj, mix, col_in, col_out, gates, norm_c, bsz, seq, h_c, lane0):
    tq = _tile(seq, 2048)
    rc = _tile(tq, 256)
    nq = seq // tq
    blk_in, blk_out = col_in // HEAD_DIM, col_out // HEAD_DIM
    prep = _tile(seq, 1024)
    return pl.pallas_call(
        functools.partial(_fox_kernel, lane0=lane0, tq=tq, rc=rc, seq=seq, prep=prep),
        grid=(bsz, h_c, nq),
        in_specs=[pl.BlockSpec((tq, HEAD_DIM), lambda b, h, qi: (b * nq + qi, blk_in + h)),
                  pl.BlockSpec((seq, HEAD_DIM), lambda b, h, qi: (b, blk_in + h_c + h)),
                  pl.BlockSpec((seq, HEAD_DIM), lambda b, h, qi: (b, blk_in + 2 * h_c + h)),
                  pl.BlockSpec((seq, LANES), lambda b, h, qi: (b, 0)),
                  pl.BlockSpec((1, HEAD_DIM), lambda b, h, qi: (0, 0)),
                  pl.BlockSpec(memory_space=pl.ANY)],
        out_specs=pl.BlockSpec((tq, HEAD_DIM), lambda b, h, qi: (b * nq + qi, blk_out + h)),
        out_shape=jax.ShapeDtypeStruct(mix.shape, mix.dtype),
        scratch_shapes=[pltpu.VMEM((seq, 2 * HEAD_DIM), BF16), pltpu.VMEM((seq, 2 * HEAD_DIM), BF16),
                        pltpu.VMEM((tq, 2 * HEAD_DIM), BF16), pltpu.VMEM((tq, LANES), F32),
                        pltpu.VMEM((tq, 2 * HEAD_DIM), F32)],
        input_output_aliases={5: 0},
        compiler_params=_params(("parallel", "parallel", "arbitrary")),
        name="fox_attention",
    )(proj, proj, proj, gates, norm_c.reshape(1, HEAD_DIM), mix)


def kernel(x, c, w_ada, b_ada, ln_g, ln_b, w_ffn_in, w_ffn_out, w_in, conv_w, a_log, dt_bias, norm_a,
           sgu_ln_g, sgu_ln_b, w_s, b_s, b_f, norm_c, w_o):
    bsz, seq, d = x.shape
    depth = w_ada.shape[0]
    m = bsz * seq
    h_a, h_c = a_log.shape[1], b_f.shape[1]
    d_a, d_c = h_a * HEAD_DIM, h_c * HEAD_DIM
    d_b = sgu_ln_g.shape[1]
    alpha = (2.0 * depth) ** 0.25
    assert 2 * h_a + h_c <= LANES and seq % DN_CHUNK == 0 and seq % w_s.shape[-1] == 0
    assert w_in.shape[-1] == 4 * d_a + 2 * h_a + 2 * d_b + 3 * d_c + h_c and d_a + d_b + d_c == d

    c_pad = jnp.zeros((SUBLANES, d), F32).at[:bsz].set(c)
    mod = _ada(c_pad, w_ada, b_ada)[:, :bsz]

    def mods(l):
        return [v.reshape(bsz, 1, d) for v in jnp.split(mod[l], 9, axis=-1)]

    w_big, w_small = _win_prep(w_in, d_a, d_b, d_c, h_a, h_c)
    w_ffn_in16 = w_ffn_in.astype(BF16)
    w_ffn_out16 = w_ffn_out.astype(BF16)
    w_o16 = w_o.astype(BF16)
    col_uv = 4 * d_a
    col_c = col_uv + 2 * d_b

    x2 = x.reshape(m, d)
    h = None
    for l in range(depth):
        sh1, sc1, ga1, sh2, sc2, ga2, sh3, sc3, ga3 = mods(l)
        if l == 0:
            h = _modulate(x2, sh1, sc1, seq)

        act = _swiglu_in(h, w_ffn_in16, (l, 0))
        y = _matmul_ktiled(act, w_ffn_out16, (l, 0), BF16, name="ffn_out")
        x2, h = _resid_ln(x2, y, ga1, ln_g[l, 0], ln_b[l, 0], (sh2, sc2), seq, alpha, 0.5)

        proj = _matmul(h, w_big, (l,), BF16, tn=1280, name="mixer_in")
        small = _matmul(h, w_small, (l,), F32, tn=LANES, name="mixer_in_small")
        gates, gates_t = _gates(small, a_log[l], dt_bias[l], b_f[l], bsz, seq)
        mix = jnp.zeros((m, d), BF16)
        mix = _deltanet(proj, mix, conv_w[l], gates, gates_t, norm_a[l], bsz, seq, h_a)
        mix = _sgu(proj, mix, col_uv, d_a, d_b, sgu_ln_g[l], sgu_ln_b[l], w_s[l], b_s[l])
        mix = _fox(proj, mix, col_c, d_a + d_b, gates, norm_c[l], bsz, seq, h_c, 2 * h_a)
        y = _matmul(mix, w_o16, (l,), BF16, name="mixer_out")
        x2, h = _resid_ln(x2, y, ga2, ln_g[l, 1], ln_b[l, 1], (sh3, sc3), seq, alpha, 1.0)

        act = _swiglu_in(h, w_ffn_in16, (l, 1))
        y = _matmul_ktiled(act, w_ffn_out16, (l, 1), BF16, name="ffn_out")
        nxt = None
        if l + 1 < depth:
            nsh, nsc = mods(l + 1)[:2]
            nxt = (nsh, nsc)
        x2, h = _resid_ln(x2, y, ga3, ln_g[l, 2], ln_b[l, 2], nxt, seq, alpha, 0.5)
    return x2.reshape(bsz, seq, d)
```
